```python
import math
import jax, jax.numpy as jnp
from jax import lax
import numpy as np

D_MODEL = 1024
BATCH = 1
SEQ = 16384
DEPTH = 2
DEC_BATCH = 32
DEC_SEQ = 8
PAST_LEN = 16384
PAGE_SIZE = 128

H_A = 8
HD_A = 64
W_A = H_A * HD_A
H_B = 4
HD_B = 64
W_B = H_B * 2 * HD_B
N_BUCKETS = 32
MAX_EXACT = 16
MAX_DISTANCE = 128
SPLIT_SIZES = (W_A, W_A, W_A, W_A, W_B, W_B, W_B, W_B, D_MODEL, D_MODEL)
D_IN = 4 * W_A + 4 * W_B + 2 * D_MODEL
Q_BLOCK = 128
RMS_EPS = 1e-6
SUBLN_EPS = 1e-5
NEG_INF = -1e30
POOL_NUM = 5
POOL_DEN = 4

kernel_name = "stickbreak_diffattn_gated_hybrid_step"


def rms_norm(x, g, eps=RMS_EPS):
    xf = x.astype(jnp.float32)
    y = xf * lax.rsqrt(jnp.mean(xf * xf, axis=-1, keepdims=True) + eps)
    return (y * g.astype(jnp.float32)).astype(x.dtype)


def t5_bucket(q_pos, k_pos):
    n = jnp.maximum(q_pos[:, None] - k_pos[None, :], 0)
    nf = jnp.maximum(n, 1).astype(jnp.float32)
    large = MAX_EXACT + (jnp.log(nf / MAX_EXACT) / math.log(MAX_DISTANCE / MAX_EXACT)
                         * (N_BUCKETS - MAX_EXACT)).astype(jnp.int32)
    large = jnp.minimum(large, N_BUCKETS - 1)
    return jnp.where(n < MAX_EXACT, n, large)


def stick_breaking_attention(q, k, v, q_pos, k_pos):
    z = jnp.einsum("bqhd,bkhd->bhqk", q, k).astype(jnp.float32) * (HD_A ** -0.5)
    mask = k_pos[None, :] < q_pos[:, None]
    log_beta = jax.nn.log_sigmoid(z)
    log_keep = jnp.where(mask, jax.nn.log_sigmoid(-z), 0.0)
    later = lax.cumsum(log_keep, axis=3, reverse=True) - log_keep
    w = jnp.where(mask, jnp.exp(log_beta + later), 0.0)
    out = jnp.einsum("bhqk,bkhd->bqhd", w, v.astype(jnp.float32))
    return out.astype(v.dtype)


def differential_attention(q, k, v, q_pos, k_pos, rel_bias, lam):
    logits = jnp.einsum("bqhcd,bkhcd->bhcqk", q, k).astype(jnp.float32) * (HD_B ** -0.5)
    bias = jnp.transpose(rel_bias[t5_bucket(q_pos, k_pos)].astype(jnp.float32), (2, 0, 1))
    mask = k_pos[None, :] <= q_pos[:, None]
    logits = jnp.where(mask, logits + bias[None, :, None], NEG_INF)
    p = jax.nn.softmax(logits, axis=-1)
    attn = p[:, :, 0] - lam.astype(jnp.float32) * p[:, :, 1]
    out = jnp.einsum("bhqk,bkhe->bqhe", attn, v.astype(jnp.float32))
    return out.astype(v.dtype)


def sweep_query_blocks(attend, q, q_pos):
    b, t = q.shape[:2]
    nb = t // Q_BLOCK
    qb = jnp.swapaxes(q.reshape((b, nb, Q_BLOCK) + q.shape[2:]), 0, 1)
    pb = q_pos.reshape(nb, Q_BLOCK)
    out = lax.map(lambda a: attend(a[0], a[1]), (qb, pb))
    return jnp.swapaxes(out, 0, 1).reshape((b, t) + out.shape[3:])


def gather_pages(pool, layer, page_table):
    g = pool[layer, page_table]
    return g.reshape((g.shape[0], g.shape[1] * g.shape[2]) + g.shape[3:])


def hybrid_layer(x, q_pos, past, norm_g, w_in, lam, lam_init, subln_g,
                 w_up_a, w_up_b, w_out, rel_bias):
    b, t, _ = x.shape
    h = rms_norm(x, norm_g)
    proj = h @ w_in
    split_at = np.cumsum(SPLIT_SIZES)[:-1].tolist()
    qa, ka, va, za, qb, kb, vb, zb, ga, gb = jnp.split(proj, split_at, axis=-1)
    qa = qa.reshape(b, t, H_A, HD_A)
    ka = ka.reshape(b, t, H_A, HD_A)
    va = va.reshape(b, t, H_A, HD_A)
    qb = qb.reshape(b, t, H_B, 2, HD_B)
    kb = kb.reshape(b, t, H_B, 2 * HD_B)
    vb = vb.reshape(b, t, H_B, 2 * HD_B)
    new_rows = (ka, va, kb, vb)
    if past is None:
        keys = new_rows
    else:
        keys = tuple(jnp.concatenate([p, n.astype(p.dtype)], axis=1) for p, n in zip(past, new_rows))
    k_sb, v_sb, k_df, v_df = keys
    k_df = k_df.reshape(k_df.shape[:3] + (2, HD_B))
    k_pos = jnp.arange(k_sb.shape[1])

    def attend_a(qq, pp):
        return stick_breaking_attention(qq, k_sb, v_sb, pp, k_pos)

    def attend_b(qq, pp):
        return differential_attention(qq, k_df, v_df, pp, k_pos, rel_bias, lam)

    if past is None:
        oa = sweep_query_blocks(attend_a, qa, q_pos)
        ob = sweep_query_blocks(attend_b, qb, q_pos)
    else:
        oa = attend_a(qa, q_pos)
        ob = attend_b(qb, q_pos)
    ob = rms_norm(ob, subln_g, SUBLN_EPS) * (1.0 - lam_init)
    oa = oa.reshape(b, t, W_A) * jax.nn.silu(za)
    ob = ob.reshape(b, t, W_B) * jax.nn.silu(zb)
    merged = jax.nn.sigmoid(ga) * (oa @ w_up_a) + jax.nn.sigmoid(gb) * (ob @ w_up_b)
    return x + merged @ w_out, new_rows


def setup_inputs(seed: int = 0) -> dict:
    key = jax.random.key(seed)
    ks = jax.random.split(key, 20)
    f32 = jnp.float32
    n_pages = PAST_LEN // PAGE_SIZE
    n_pool = (POOL_NUM * DEC_BATCH * n_pages) // POOL_DEN
    nrm = jax.random.normal
    page_table = jax.random.permutation(ks[6], n_pool)[: DEC_BATCH * n_pages]
    page_table = page_table.reshape(DEC_BATCH, n_pages).astype(jnp.int32)
    return {
        "x_prompt": nrm(ks[0], (BATCH, SEQ, D_MODEL), f32),
        "x_sample": nrm(ks[1], (DEC_BATCH, DEC_SEQ, D_MODEL), f32),
        "cache_sb_k": nrm(ks[2], (DEPTH, n_pool, PAGE_SIZE, H_A, HD_A), f32),
        "cache_sb_v": nrm(ks[3], (DEPTH, n_pool, PAGE_SIZE, H_A, HD_A), f32),
        "cache_diff_k": nrm(ks[4], (DEPTH, n_pool, PAGE_SIZE, H_B, 2 * HD_B), f32),
        "cache_diff_v": nrm(ks[5], (DEPTH, n_pool, PAGE_SIZE, H_B, 2 * HD_B), f32),
        "page_table": page_table,
        "norm_g": 1.0 + 0.02 * nrm(ks[7], (DEPTH, D_MODEL), f32),
        "w_in": nrm(ks[8], (DEPTH, D_MODEL, D_IN), f32) * D_MODEL ** -0.5,
        "lambda_q1": 0.1 * nrm(ks[9], (DEPTH, HD_B), f32),
        "lambda_k1": 0.1 * nrm(ks[10], (DEPTH, HD_B), f32),
        "lambda_q2": 0.1 * nrm(ks[11], (DEPTH, HD_B), f32),
        "lambda_k2": 0.1 * nrm(ks[12], (DEPTH, HD_B), f32),
        "subln_g": 1.0 + 0.02 * nrm(ks[13], (DEPTH, 2 * HD_B), f32),
        "w_up_a": nrm(ks[14], (DEPTH, W_A, D_MODEL), f32) * W_A ** -0.5,
        "w_up_b": nrm(ks[15], (DEPTH, W_B, D_MODEL), f32) * W_B ** -0.5,
        "w_out": nrm(ks[16], (DEPTH, D_MODEL, D_MODEL), f32) * D_MODEL ** -0.5,
        "rel_bias": 0.5 * nrm(ks[17], (N_BUCKETS, H_B), f32),
        "final_norm_g": 1.0 + 0.02 * nrm(ks[18], (D_MODEL,), f32),
    }


def reference(x_prompt, x_sample, cache_sb_k, cache_sb_v, cache_diff_k, cache_diff_v,
              page_table, norm_g, w_in, lambda_q1, lambda_k1, lambda_q2, lambda_k2,
              subln_g, w_up_a, w_up_b, w_out, rel_bias, final_norm_g):
    past_len = page_table.shape[1] * PAGE_SIZE
    pos_p = jnp.arange(x_prompt.shape[1])
    pos_s = past_len + jnp.arange(x_sample.shape[1])
    xp, xs = x_prompt, x_sample
    rows_p = ([], [], [], [])
    rows_s = ([], [], [], [])
    for i in range(DEPTH):
        lam_init = 0.8 - 0.6 * math.exp(-0.3 * i)
        lam = (jnp.exp(jnp.sum(lambda_q1[i].astype(jnp.float32) * lambda_k1[i].astype(jnp.float32)))
               - jnp.exp(jnp.sum(lambda_q2[i].astype(jnp.float32) * lambda_k2[i].astype(jnp.float32)))
               + lam_init)
        params = (norm_g[i], w_in[i], lam, lam_init, subln_g[i], w_up_a[i], w_up_b[i], w_out[i], rel_bias)
        xp, new_p = hybrid_layer(xp, pos_p, None, *params)
        past = (gather_pages(cache_sb_k, i, page_table), gather_pages(cache_sb_v, i, page_table),
                gather_pages(cache_diff_k, i, page_table), gather_pages(cache_diff_v, i, page_table))
        xs, new_s = hybrid_layer(xs, pos_s, past, *params)
        for lst, r in zip(rows_p, new_p):
            lst.append(r)
        for lst, r in zip(rows_s, new_s):
            lst.append(r)
    y_prompt = rms_norm(xp, final_norm_g)
    y_sample = rms_norm(xs, final_norm_g)
    return (y_prompt, y_sample,
            jnp.stack(rows_p[0]), jnp.stack(rows_p[1]), jnp.stack(rows_p[2]), jnp.stack(rows_p[3]),
            jnp.stack(rows_s[0]), jnp.stack(rows_s[1]), jnp.stack(rows_s[2]), jnp.stack(rows_s[3]))
```

```python
import functools
import math

import jax
import jax.numpy as jnp
from jax import lax
from jax.experimental import pallas as pl
from jax.experimental.pallas import tpu as pltpu

F32 = jnp.float32
BF16 = jnp.bfloat16

D_MODEL = 1024
H_A, HD_A = 8, 64
H_B, HD_B = 4, 64
W_A = H_A * HD_A
W_B = H_B * 2 * HD_B
N_BUCKETS, MAX_EXACT, MAX_DISTANCE = 32, 16, 128
RMS_EPS, SUBLN_EPS = 1e-6, 1e-5
NEG_INF = -1e30
PAGE_SIZE = 128
QK_SCALE = 0.125

LANES = 128
HALF = 64
VMEM_LIMIT = 48 * 1024 * 1024

QA0, KA0, VA0, QB0, KB0, VB0 = 0, 4, 8, 12, 16, 20
N_GROUPS = W_A // LANES

PROMPT_BLOCK = 256
PAGES_PER_STEP = 8
DEC_ROWS = 64


def _silu(z):
    return z / (1.0 + jnp.exp(-z))


def _sigmoid(z):
    return 1.0 / (1.0 + jnp.exp(-z))


def _inproj_kernel(x_ref, g_ref, w_ref, qkv_ref, ka_ref, va_ref, kb_ref, vb_ref,
                   za_ref, zb_ref, ga_ref, gb_ref):
    x = x_ref[...]
    ms = jnp.mean(x * x, axis=-1, keepdims=True)
    h = ((x * lax.rsqrt(ms + RMS_EPS)) * g_ref[...]).astype(BF16)

    def mm(c0, width):
        return jnp.dot(h, w_ref[:, c0:c0 + width], preferred_element_type=F32)

    qdt = qkv_ref.dtype
    qkv_ref[:, 0:512] = (mm(0, 512) * QK_SCALE).astype(qdt)
    ka = mm(512, 512)
    ka_ref[...] = ka
    qkv_ref[:, 512:1024] = ka.astype(qdt)
    va = mm(1024, 512)
    va_ref[...] = va
    qkv_ref[:, 1024:1536] = va.astype(qdt)
    za_ref[...] = mm(1536, 512)
    qkv_ref[:, 1536:2048] = (mm(2048, 512) * QK_SCALE).astype(qdt)
    kb = mm(2560, 512)
    kb_ref[...] = kb
    qkv_ref[:, 2048:2560] = kb.astype(qdt)
    vb = mm(3072, 512)
    vb_ref[...] = vb
    qkv_ref[:, 2560:3072] = vb.astype(qdt)
    zb_ref[...] = mm(3584, 512)
    ga_ref[...] = mm(4096, 1024)
    gb_ref[...] = mm(5120, 1024)


def _inproj(x, g, w_bf16, qkv_dtype, tm):
    t = x.shape[0]
    d_in = w_bf16.shape[1]
    row = lambda w: pl.BlockSpec((tm, w), lambda i: (i, 0))
    out_shape = [jax.ShapeDtypeStruct((t, 3072), qkv_dtype)]
    out_shape += [jax.ShapeDtypeStruct((t, 512), F32)] * 6
    out_shape += [jax.ShapeDtypeStruct((t, 1024), F32)] * 2
    return pl.pallas_call(
        _inproj_kernel,
        grid=(t // tm,),
        in_specs=[row(D_MODEL),
                  pl.BlockSpec((1, D_MODEL), lambda i: (0, 0)),
                  pl.BlockSpec((D_MODEL, d_in), lambda i: (0, 0))],
        out_specs=[row(3072)] + [row(512)] * 6 + [row(1024)] * 2,
        out_shape=out_shape,
        compiler_params=pltpu.CompilerParams(
            dimension_semantics=("arbitrary",), vmem_limit_bytes=VMEM_LIMIT),
        name="inproj",
    )(x, g, w_bf16)


def _sb_tile(z, mask, pv, tri_ref, acc_ref, car_ref):
    sp = jnp.log(1.0 + jnp.exp(-jnp.abs(z)))
    lb = jnp.minimum(z, 0.0) - sp
    lk = lb - z
    if mask is not None:
        lk = jnp.where(mask, lk, 0.0)
    hi = lk.astype(BF16)
    lo = (lk - hi.astype(F32)).astype(BF16)
    tri = tri_ref[...]
    later = (jnp.dot(hi, tri, preferred_element_type=F32)
             + jnp.dot(lo, tri, preferred_element_type=F32))
    w = jnp.exp(lb + later + car_ref[...])
    if mask is not None:
        w = jnp.where(mask, w, 0.0)
    acc_ref[...] += pv(w)
    car_ref[...] += jnp.sum(lk, axis=-1, keepdims=True)


def _softmax_tile(s, pv, acc_ref, m_ref, l_ref):
    m_old = m_ref[...]
    m_new = jnp.maximum(m_old, jnp.max(s, axis=-1, keepdims=True))
    alpha = jnp.exp(m_old - m_new)
    p = jnp.exp(s - m_new)
    l_ref[...] = alpha * l_ref[...] + jnp.sum(p, axis=-1, keepdims=True)
    acc_ref[...] = alpha * acc_ref[...] + pv(p)
    m_ref[...] = m_new


def _pv_bf16(v_bf16):
    return lambda w: jnp.dot(w.astype(BF16), v_bf16, preferred_element_type=F32)


def _qk(q_bf16, k_bf16):
    return lax.dot_general(q_bf16, k_bf16, (((1,), (1,)), ((), ())), preferred_element_type=F32)


def _subln(o, g, scale):
    ms = jnp.mean(o * o, axis=-1, keepdims=True)
    return ((o * lax.rsqrt(ms + SUBLN_EPS)) * g) * scale


def _sb_prompt_kernel(q_ref, k_ref, v_ref, z_ref, tri_ref, o_ref, acc_ref, car_ref):
    i = pl.program_id(1)
    blk = q_ref.shape[0]
    lane = lax.broadcasted_iota(jnp.int32, (blk, LANES), 1)
    row = lax.broadcasted_iota(jnp.int32, (blk, blk), 0)
    col = lax.broadcasted_iota(jnp.int32, (blk, blk), 1)
    q = q_ref[...]
    outs = []
    for c in range(2):
        qc = jnp.where((lane >= HALF * c) & (lane < HALF * (c + 1)), q, jnp.zeros_like(q))
        acc_ref[...] = jnp.zeros_like(acc_ref)
        car_ref[...] = jnp.zeros_like(car_ref)

        def step(kb, mask):
            start = pl.multiple_of(kb * blk, blk)
            k = k_ref[pl.ds(start, blk), :]
            v = v_ref[pl.ds(start, blk), :]
            _sb_tile(_qk(qc, k), mask, _pv_bf16(v), tri_ref, acc_ref, car_ref)

        step(i, col < row)

        def body(j, carry):
            step(i - 1 - j, None)
            return carry

        lax.fori_loop(0, i, body, 0)
        outs.append(acc_ref[...])
    o = jnp.where(lane < HALF, outs[0], outs[1])
    o_ref[...] = (o * _silu(z_ref[...])).astype(o_ref.dtype)


def _sb_prompt(qkv, za, tri):
    t = qkv.shape[0]
    blk = PROMPT_BLOCK
    return pl.pallas_call(
        _sb_prompt_kernel,
        grid=(N_GROUPS, t // blk),
        in_specs=[pl.BlockSpec((blk, LANES), lambda g, i: (i, QA0 + g)),
                  pl.BlockSpec((t, LANES), lambda g, i: (0, KA0 + g)),
                  pl.BlockSpec((t, LANES), lambda g, i: (0, VA0 + g)),
                  pl.BlockSpec((blk, LANES), lambda g, i: (i, g)),
                  pl.BlockSpec((blk, blk), lambda g, i: (0, 0))],
        out_specs=pl.BlockSpec((blk, LANES), lambda g, i: (i, g)),
        out_shape=jax.ShapeDtypeStruct((t, W_A), BF16),
        scratch_shapes=[pltpu.VMEM((blk, LANES), F32), pltpu.VMEM((blk, 1), F32)],
        compiler_params=pltpu.CompilerParams(
            dimension_semantics=("arbitrary", "arbitrary"), vmem_limit_bytes=VMEM_LIMIT),
        name="sb_prompt",
    )(qkv, qkv, qkv, za, tri)


def _diff_prompt_kernel(sc_ref, q_ref, k_ref, v_ref, z_ref, bias_ref, g_ref, o_ref,
                        acc_ref, m_ref, l_ref):
    h = pl.program_id(0)
    i = pl.program_id(1)
    blk = q_ref.shape[0]
    lane = lax.broadcasted_iota(jnp.int32, (blk, LANES), 1)
    row = lax.broadcasted_iota(jnp.int32, (blk, blk), 0)
    col = lax.broadcasted_iota(jnp.int32, (blk, blk), 1)
    q = q_ref[...]
    far = sc_ref[2 + h]
    outs = []
    for c in range(2):
        qc = jnp.where((lane >= HALF * c) & (lane < HALF * (c + 1)), q, jnp.zeros_like(q))
        acc_ref[...] = jnp.zeros_like(acc_ref)
        l_ref[...] = jnp.zeros_like(l_ref)
        m_ref[...] = jnp.full_like(m_ref, NEG_INF)

        def scores(kb):
            start = pl.multiple_of(kb * blk, blk)
            k = k_ref[pl.ds(start, blk), :]
            v = v_ref[pl.ds(start, blk), :]
            return _qk(qc, k), v

        s, v = scores(i)
        s = jnp.where(col <= row, s + bias_ref[0, 0], NEG_INF)
        _softmax_tile(s, _pv_bf16(v), acc_ref, m_ref, l_ref)

        @pl.when(i > 0)
        def _():
            s1, v1 = scores(i - 1)
            _softmax_tile(s1 + bias_ref[0, 1], _pv_bf16(v1), acc_ref, m_ref, l_ref)

        def body(j, carry):
            s2, v2 = scores(i - 2 - j)
            _softmax_tile(s2 + far, _pv_bf16(v2), acc_ref, m_ref, l_ref)
            return carry

        lax.fori_loop(0, jnp.maximum(i - 1, 0), body, 0)
        outs.append(acc_ref[...] / l_ref[...])
    o = outs[0] - sc_ref[0] * outs[1]
    o = _subln(o, g_ref[...], sc_ref[1])
    o_ref[...] = (o * _silu(z_ref[...])).astype(o_ref.dtype)


def _diff_prompt(scalars, qkv, zb, bias_near, subln_g):
    t = qkv.shape[0]
    blk = PROMPT_BLOCK
    grid_spec = pltpu.PrefetchScalarGridSpec(
        num_scalar_prefetch=1,
        grid=(H_B, t // blk),
        in_specs=[pl.BlockSpec((blk, LANES), lambda h, i, sc: (i, QB0 + h)),
                  pl.BlockSpec((t, LANES), lambda h, i, sc: (0, KB0 + h)),
                  pl.BlockSpec((t, LANES), lambda h, i, sc: (0, VB0 + h)),
                  pl.BlockSpec((blk, LANES), lambda h, i, sc: (i, h)),
                  pl.BlockSpec((1, 2, blk, blk), lambda h, i, sc: (h, 0, 0, 0)),
                  pl.BlockSpec((1, LANES), lambda h, i, sc: (0, 0))],
        out_specs=pl.BlockSpec((blk, LANES), lambda h, i, sc: (i, h)),
        scratch_shapes=[pltpu.VMEM((blk, LANES), F32), pltpu.VMEM((blk, 1), F32),
                        pltpu.VMEM((blk, 1), F32)],
    )
    return pl.pallas_call(
        _diff_prompt_kernel,
        grid_spec=grid_spec,
        out_shape=jax.ShapeDtypeStruct((t, W_B), BF16),
        compiler_params=pltpu.CompilerParams(
            dimension_semantics=("arbitrary", "arbitrary"), vmem_limit_bytes=VMEM_LIMIT),
        name="diff_prompt",
    )(scalars, qkv, qkv, qkv, zb, bias_near, subln_g)


def _block_diag_queries(q):
    n = q.shape[0]
    qt = jnp.concatenate([q] * (DEC_ROWS // n), axis=0)
    r = lax.broadcasted_iota(jnp.int32, qt.shape, 0)
    l = lax.broadcasted_iota(jnp.int32, qt.shape, 1)
    return jnp.where((l // HALF) == (r // n), qt, 0.0)


def _fold_rows(x, lanes_per_row_group, rows_per_group):
    r = lax.broadcasted_iota(jnp.int32, x.shape, 0)
    l = lax.broadcasted_iota(jnp.int32, x.shape, 1)
    x = jnp.where((l // lanes_per_row_group) == (r // rows_per_group), x, 0.0)
    out = x[0:8]
    for a in range(1, x.shape[0] // 8):
        out = out + x[8 * a:8 * a + 8]
    return out


def _sb_decode_kernel(pt_ref, q_ref, kn_ref, vn_ref, z_ref, tri_ref, *rest):
    npg = PAGES_PER_STEP
    kt_refs, vt_refs = rest[:npg], rest[npg:2 * npg]
    o_ref, qbd_ref, acc_ref, car_ref, pad_ref = rest[2 * npg:]
    j = pl.program_id(1)
    n_new = kn_ref.shape[1]

    @pl.when(j == 0)
    def _():
        qbd_ref[...] = _block_diag_queries(q_ref[0])
        acc_ref[...] = jnp.zeros_like(acc_ref)
        car_ref[...] = jnp.zeros_like(car_ref)
        r = lax.broadcasted_iota(jnp.int32, (DEC_ROWS, PAGE_SIZE), 0)
        s = lax.broadcasted_iota(jnp.int32, (DEC_ROWS, PAGE_SIZE), 1)
        mask = s < (r % n_new)
        pad_ref[...] = jnp.zeros_like(pad_ref)
        pad_ref[0:n_new, :] = kn_ref[0]
        z = _qk(qbd_ref[...], pad_ref[...])
        pad_ref[0:n_new, :] = vn_ref[0]
        vpad = pad_ref[...]
        _sb_tile(z, mask, lambda w: jnp.dot(w, vpad, preferred_element_type=F32),
                 tri_ref, acc_ref, car_ref)

    for p in range(npg):
        z = jnp.dot(qbd_ref[...], kt_refs[p][...], preferred_element_type=F32)
        _sb_tile(z, None, lambda w, p=p: _qk(w, vt_refs[p][...]), tri_ref, acc_ref, car_ref)

    @pl.when(j == pl.num_programs(1) - 1)
    def _():
        o = _fold_rows(acc_ref[...], HALF, n_new)
        o_ref[0] = o * _silu(z_ref[0])


def _page_spec(layer, n_pages, p):
    def index_map(b, j, pt, *_):
        page = n_pages - 1 - (j * PAGES_PER_STEP + p)
        return (layer, pt[b * n_pages + page], 0, 0)
    return pl.BlockSpec((None, None, 512, PAGE_SIZE), index_map)


def _sb_decode(pt_flat, n_pages, layer, q3, kn3, vn3, za3, tri, cache_k, cache_v):
    nb, n_new, _ = kn3.shape
    per_req = lambda w, cb: pl.BlockSpec((1, n_new, w), lambda b, j, pt: (b, 0, cb))
    grid_spec = pltpu.PrefetchScalarGridSpec(
        num_scalar_prefetch=1,
        grid=(nb, n_pages // PAGES_PER_STEP),
        in_specs=[per_req(512, QA0 // N_GROUPS), per_req(512, 0), per_req(512, 0), per_req(512, 0),
                  pl.BlockSpec((PAGE_SIZE, PAGE_SIZE), lambda b, j, pt: (0, 0))]
                 + [_page_spec(layer, n_pages, p) for p in range(PAGES_PER_STEP)] * 2,
        out_specs=per_req(512, 0),
        scratch_shapes=[pltpu.VMEM((DEC_ROWS, 512), F32), pltpu.VMEM((DEC_ROWS, 512), F32),
                        pltpu.VMEM((DEC_ROWS, 1), F32), pltpu.VMEM((PAGE_SIZE, 512), F32)],
    )
    return pl.pallas_call(
        _sb_decode_kernel,
        grid_spec=grid_spec,
        out_shape=jax.ShapeDtypeStruct((nb, n_new, W_A), F32),
        compiler_params=pltpu.CompilerParams(
            dimension_semantics=("arbitrary", "arbitrary"), vmem_limit_bytes=VMEM_LIMIT),
        name="sb_decode",
    )(pt_flat, q3, kn3, vn3, za3, tri, *([cache_k] * PAGES_PER_STEP), *([cache_v] * PAGES_PER_STEP))


def _diff_decode_kernel(pt_ref, sc_ref, q_ref, kn_ref, vn_ref, z_ref, bias_ref, g_ref, *rest):
    npg = PAGES_PER_STEP
    k_refs, v_refs = rest[:npg], rest[npg:2 * npg]
    o_ref, qbd_ref, acc_ref, m_ref, l_ref, pad_ref = rest[2 * npg:]
    j = pl.program_id(1)
    n_new = kn_ref.shape[1]
    hrows = 2 * n_new

    def scores(key_rows_of_head):
        return jnp.concatenate(
            [_qk(qbd_ref[hrows * h:hrows * (h + 1), :], key_rows_of_head(h)) for h in range(H_B)], axis=0)

    def pv(value_rows_of_head):
        return lambda p: jnp.concatenate(
            [jnp.dot(p[hrows * h:hrows * (h + 1)], value_rows_of_head(h), preferred_element_type=F32)
             for h in range(H_B)], axis=0)

    @pl.when(j == 0)
    def _():
        q = q_ref[0]
        lane = lax.broadcasted_iota(jnp.int32, (n_new, LANES), 1)
        for h in range(H_B):
            qh = q[:, LANES * h:LANES * (h + 1)]
            for c in range(2):
                qbd_ref[hrows * h + n_new * c:hrows * h + n_new * (c + 1), :] = jnp.where(
                    (lane // HALF) == c, qh, 0.0)
        acc_ref[...] = jnp.zeros_like(acc_ref)
        l_ref[...] = jnp.zeros_like(l_ref)
        m_ref[...] = jnp.full_like(m_ref, NEG_INF)
        r = lax.broadcasted_iota(jnp.int32, (DEC_ROWS, PAGE_SIZE), 0)
        s = lax.broadcasted_iota(jnp.int32, (DEC_ROWS, PAGE_SIZE), 1)
        mask = s <= (r % n_new)
        pad_ref[...] = jnp.zeros_like(pad_ref)
        pad_ref[0:n_new, :] = kn_ref[0]
        sn = scores(lambda h: pad_ref[:, LANES * h:LANES * (h + 1)])
        sn = jnp.where(mask, sn + bias_ref[0], NEG_INF)
        pad_ref[0:n_new, :] = vn_ref[0]
        _softmax_tile(sn, pv(lambda h: pad_ref[:, LANES * h:LANES * (h + 1)]), acc_ref, m_ref, l_ref)

    for p in range(npg):
        s = scores(lambda h, p=p: k_refs[p][pl.ds(h, PAGE_SIZE, stride=H_B), :])
        if p == 0:
            s = s + jnp.where(j == 0, bias_ref[1], bias_ref[2])
        else:
            s = s + bias_ref[2]
        _softmax_tile(s, pv(lambda h, p=p: v_refs[p][pl.ds(h, PAGE_SIZE, stride=H_B), :]),
                      acc_ref, m_ref, l_ref)

    @pl.when(j == pl.num_programs(1) - 1)
    def _():
        o = acc_ref[...] / l_ref[...]
        g = g_ref[...]
        parts = []
        for h in range(H_B):
            oh = o[hrows * h:hrows * h + n_new] - sc_ref[0] * o[hrows * h + n_new:hrows * (h + 1)]
            parts.append(_subln(oh, g, sc_ref[1]))
        o_ref[0] = jnp.concatenate(parts, axis=-1) * _silu(z_ref[0])


def _diff_decode(pt_flat, scalars, n_pages, layer, q3, kn3, vn3, zb3, bias_dec, subln_g,
                 cache_k, cache_v):
    nb, n_new, _ = kn3.shape
    per_req = lambda w, cb: pl.BlockSpec((1, n_new, w), lambda b, j, pt, sc: (b, 0, cb))

    grid_spec = pltpu.PrefetchScalarGridSpec(
        num_scalar_prefetch=2,
        grid=(nb, n_pages // PAGES_PER_STEP),
        in_specs=[per_req(512, QB0 // N_GROUPS), per_req(512, 0), per_req(512, 0), per_req(512, 0),
                  pl.BlockSpec((3, DEC_ROWS, PAGE_SIZE), lambda b, j, pt, sc: (0, 0, 0)),
                  pl.BlockSpec((1, LANES), lambda b, j, pt, sc: (0, 0))]
                 + [_page_spec(layer, n_pages, p) for p in range(PAGES_PER_STEP)] * 2,
        out_specs=per_req(512, 0),
        scratch_shapes=[pltpu.VMEM((DEC_ROWS, LANES), F32), pltpu.VMEM((DEC_ROWS, LANES), F32),
                        pltpu.VMEM((DEC_ROWS, 1), F32), pltpu.VMEM((DEC_ROWS, 1), F32),
                        pltpu.VMEM((PAGE_SIZE, 512), F32)],
    )
    return pl.pallas_call(
        _diff_decode_kernel,
        grid_spec=grid_spec,
        out_shape=jax.ShapeDtypeStruct((nb, n_new, W_B), F32),
        compiler_params=pltpu.CompilerParams(
            dimension_semantics=("arbitrary", "arbitrary"), vmem_limit_bytes=VMEM_LIMIT),
        name="diff_decode",
    )(pt_flat, scalars, q3, kn3, vn3, zb3, bias_dec, subln_g,
      *([cache_k] * PAGES_PER_STEP), *([cache_v] * PAGES_PER_STEP))


def _merge_kernel(x_ref, oa_ref, ob_ref, ga_ref, gb_ref, wa_ref, wb_ref, wo_ref, fg_ref, y_ref,
                  *, final):
    a = jnp.dot(oa_ref[...].astype(BF16), wa_ref[...], preferred_element_type=F32)
    b = jnp.dot(ob_ref[...].astype(BF16), wb_ref[...], preferred_element_type=F32)
    merged = _sigmoid(ga_ref[...]) * a + _sigmoid(gb_ref[...]) * b
    y = x_ref[...] + jnp.dot(merged.astype(BF16), wo_ref[...], preferred_element_type=F32)
    if final:
        ms = jnp.mean(y * y, axis=-1, keepdims=True)
        y = (y * lax.rsqrt(ms + RMS_EPS)) * fg_ref[...]
    y_ref[...] = y


def _merge(x, oa, ob, ga, gb, wa, wb, wo, fg, final, tm):
    t = x.shape[0]
    row = lambda w: pl.BlockSpec((tm, w), lambda i: (i, 0))
    whole = lambda a: pl.BlockSpec(a.shape, lambda i: (0, 0))
    return pl.pallas_call(
        functools.partial(_merge_kernel, final=final),
        grid=(t // tm,),
        in_specs=[row(D_MODEL), row(W_A), row(W_B), row(D_MODEL), row(D_MODEL),
                  whole(wa), whole(wb), whole(wo), whole(fg)],
        out_specs=row(D_MODEL),
        out_shape=jax.ShapeDtypeStruct((t, D_MODEL), F32),
        compiler_params=pltpu.CompilerParams(
            dimension_semantics=("arbitrary",), vmem_limit_bytes=VMEM_LIMIT),
        name="merge",
    )(x, oa, ob, ga, gb, wa, wb, wo, fg)


def _bias_by_distance(rel_bias, n):
    d = jnp.arange(n)
    nf = jnp.maximum(d, 1).astype(F32)
    large = MAX_EXACT + (jnp.log(nf / MAX_EXACT) / math.log(MAX_DISTANCE / MAX_EXACT)
                         * (N_BUCKETS - MAX_EXACT)).astype(jnp.int32)
    large = jnp.minimum(large, N_BUCKETS - 1)
    return rel_bias[jnp.where(d < MAX_EXACT, d, large)].astype(F32)


def _strict_lower(n):
    j = lax.broadcasted_iota(jnp.int32, (n, n), 0)
    s = lax.broadcasted_iota(jnp.int32, (n, n), 1)
    return (j > s).astype(BF16)


def kernel(x_prompt, x_sample, cache_sb_k, cache_sb_v, cache_diff_k, cache_diff_v, page_table,
           norm_g, w_in, lambda_q1, lambda_k1, lambda_q2, lambda_k2, subln_g, w_up_a, w_up_b,
           w_out, rel_bias, final_norm_g):
    depth = w_in.shape[0]
    nb_p, t_p, _ = x_prompt.shape
    nb_s, n_new, _ = x_sample.shape
    n_pages = page_table.shape[1]
    n_pool = cache_sb_k.shape[1]
    blk = PROMPT_BLOCK
    assert nb_p == 1 and t_p % blk == 0 and blk > MAX_DISTANCE
    assert n_new * H_A == DEC_ROWS and n_new * 2 * H_B == DEC_ROWS
    assert n_pages % PAGES_PER_STEP == 0 and PAGE_SIZE >= MAX_DISTANCE

    xp = x_prompt.reshape(t_p, D_MODEL)
    xs = x_sample.reshape(nb_s * n_new, D_MODEL)
    pt_flat = page_table.reshape(-1).astype(jnp.int32)
    caches = [jnp.transpose(c, (0, 1, 3, 4, 2)).reshape(depth, n_pool, 512, PAGE_SIZE)
              for c in (cache_sb_k, cache_sb_v)]
    caches += [c.reshape(depth, n_pool, 512, PAGE_SIZE) for c in (cache_diff_k, cache_diff_v)]
    tri_p = _strict_lower(blk)
    tri_s = _strict_lower(PAGE_SIZE)
    fg = final_norm_g.reshape(1, D_MODEL).astype(F32)

    tab = _bias_by_distance(rel_bias, 2 * blk + 1)
    far = tab[-1]
    r = jnp.arange(blk)
    near = jnp.stack([tab[jnp.maximum(r[:, None] - r[None, :], 0)],
                      tab[blk + r[:, None] - r[None, :]]])
    bias_near = jnp.transpose(near, (3, 0, 1, 2))
    rr = jnp.arange(DEC_ROWS)
    qi = rr % n_new
    hh = rr // (2 * n_new)
    ss = jnp.arange(PAGE_SIZE)
    bias_new = tab[jnp.maximum(qi[:, None] - ss[None, :], 0), hh[:, None]]
    bias_last = tab[PAGE_SIZE + qi[:, None] - ss[None, :], hh[:, None]]
    bias_far = jnp.broadcast_to(far[hh][:, None], (DEC_ROWS, PAGE_SIZE))
    bias_dec = jnp.stack([bias_new, bias_last, bias_far]).astype(F32)

    rows_p = ([], [], [], [])
    rows_s = ([], [], [], [])
    for i in range(depth):
        lam_init = 0.8 - 0.6 * math.exp(-0.3 * i)
        lam = (jnp.exp(jnp.sum(lambda_q1[i].astype(F32) * lambda_k1[i].astype(F32)))
               - jnp.exp(jnp.sum(lambda_q2[i].astype(F32) * lambda_k2[i].astype(F32)))
               + lam_init)
        scalars = jnp.concatenate([jnp.stack([lam, jnp.asarray(1.0 - lam_init, F32)]), far]).astype(F32)
        g_in = norm_g[i].reshape(1, D_MODEL).astype(F32)
        w_bf = w_in[i].astype(BF16)
        wa, wb, wo = w_up_a[i].astype(BF16), w_up_b[i].astype(BF16), w_out[i].astype(BF16)
        sg = subln_g[i].reshape(1, LANES).astype(F32)
        final = i == depth - 1

        qkv, ka, va, kb, vb, za, zb, ga, gb = _inproj(xp, g_in, w_bf, BF16, 256)
        oa = _sb_prompt(qkv, za, tri_p)
        ob = _diff_prompt(scalars, qkv, zb, bias_near, sg)
        xp = _merge(xp, oa, ob, ga, gb, wa, wb, wo, fg, final, 512)
        for lst, a, hn in zip(rows_p, (ka, va, kb, vb), (H_A, H_A, H_B, H_B)):
            lst.append(a.reshape(nb_p, t_p, hn, 512 // hn))

        qkv, ka, va, kb, vb, za, zb, ga, gb = _inproj(xs, g_in, w_bf, F32, nb_s * n_new)
        r3 = lambda a: a.reshape(nb_s, n_new, a.shape[-1])
        oa = _sb_decode(pt_flat, n_pages, i, r3(qkv), r3(ka), r3(va), r3(za), tri_s,
                        caches[0], caches[1])
        ob = _diff_decode(pt_flat, scalars, n_pages, i, r3(qkv), r3(kb), r3(vb), r3(zb), bias_dec, sg,
                          caches[2], caches[3])
        xs = _merge(xs, oa.reshape(-1, W_A), ob.reshape(-1, W_B), ga, gb, wa, wb, wo, fg, final,
                    nb_s * n_new)
        for lst, a, hn in zip(rows_s, (ka, va, kb, vb), (H_A, H_A, H_B, H_B)):
            lst.append(a.reshape(nb_s, n_new, hn, 512 // hn))

    y_prompt = xp.reshape(nb_p, t_p, D_MODEL)
    y_sample = xs.reshape(nb_s, n_new, D_MODEL)
    return (y_prompt, y_sample,
            jnp.stack(rows_p[0]), jnp.stack(rows_p[1]), jnp.stack(rows_p[2]), jnp.stack(rows_p[3]),
            jnp.stack(rows_s[0]), jnp.stack(rows_s[1]), jnp.stack(rows_s[2]), jnp.stack(rows_s[3]))
```

```python
import functools
import math

import jax
import jax.numpy as jnp
from jax import lax
from jax.experimental import pallas as pl
from jax.experimental.pallas import tpu as pltpu

F32 = jnp.float32
BF16 = jnp.bfloat16

D_MODEL = 1024
H_A, HD_A = 8, 64
H_B, HD_B = 4, 64
W_A = H_A * HD_A
W_B = H_B * 2 * HD_B
N_BUCKETS, MAX_EXACT, MAX_DISTANCE = 32, 16, 128
RMS_EPS, SUBLN_EPS = 1e-6, 1e-5
NEG_INF = -1e30
PAGE_SIZE = 128
QK_SCALE = 0.125

LANES = 128
HALF = 64
VMEM_LIMIT = 48 * 1024 * 1024

QA0, KA0, QB0, KB0 = 0, 4, 8, 12
N_GROUPS = W_A // LANES

Q_BLK = 512
K_BLK = 256
FAR_GROUP = 4
PAGES_PER_STEP = 8
PAGE_PAIRS = PAGES_PER_STEP // 2
DEC_ROWS = 64


def _silu(z):
    return z / (1.0 + jnp.exp(-z))


def _sigmoid(z):
    return 1.0 / (1.0 + jnp.exp(-z))


def _softplus(z):
    neg_abs = lax.bitcast_convert_type(
        lax.bitcast_convert_type(z, jnp.uint32) | jnp.uint32(0x80000000), F32)
    return jnp.maximum(z, 0.0) + jnp.log(1.0 + jnp.exp(neg_abs))


def _log_sigmoids(z):
    sp = jnp.log(1.0 + jnp.exp(-jnp.abs(z)))
    lb = jnp.minimum(z, 0.0) - sp
    return lb, lb - z


def _split_bf16(x):
    hi = x.astype(BF16)
    return hi, (x - hi.astype(F32)).astype(BF16)


def _nt(a, b):
    return lax.dot_general(a, b, (((1,), (1,)), ((), ())), preferred_element_type=F32)


def _subln(o, g, scale):
    ms = jnp.mean(o * o, axis=-1, keepdims=True)
    return ((o * lax.rsqrt(ms + SUBLN_EPS)) * g) * scale


def _inproj_kernel(x_ref, g_ref, w_ref, *outs, prompt):
    if prompt:
        qk_ref, vat_ref, vbt_ref, ka_ref, va_ref, kb_ref, vb_ref, za_ref, zb_ref, ga_ref, gb_ref = outs
    else:
        qk_ref, ka_ref, va_ref, kb_ref, vb_ref, za_ref, zb_ref, ga_ref, gb_ref = outs
    x = x_ref[...]
    ms = jnp.mean(x * x, axis=-1, keepdims=True)
    h = ((x * lax.rsqrt(ms + RMS_EPS)) * g_ref[...]).astype(BF16)

    def mm(c0, width):
        return jnp.dot(h, w_ref[:, c0:c0 + width], preferred_element_type=F32)

    qdt = qk_ref.dtype
    qk_ref[:, 0:512] = (mm(0, 512) * QK_SCALE).astype(qdt)
    ka = mm(512, 512)
    ka_ref[...] = ka
    qk_ref[:, 512:1024] = ka.astype(qdt)
    va = mm(1024, 512)
    va_ref[...] = va
    za_ref[...] = mm(1536, 512)
    qk_ref[:, 1024:1536] = (mm(2048, 512) * QK_SCALE).astype(qdt)
    kb = mm(2560, 512)
    kb_ref[...] = kb
    qk_ref[:, 1536:2048] = kb.astype(qdt)
    vb = mm(3072, 512)
    vb_ref[...] = vb
    zb_ref[...] = mm(3584, 512)
    ga_ref[...] = mm(4096, 1024)
    gb_ref[...] = mm(5120, 1024)
    if prompt:
        vat_ref[0] = va.T.astype(BF16)
        vbt_ref[0] = vb.T.astype(BF16)


def _inproj(x, g, w_bf16, prompt, tm):
    t = x.shape[0]
    d_in = w_bf16.shape[1]
    row = lambda w: pl.BlockSpec((tm, w), lambda i: (i, 0))
    out_shape = [jax.ShapeDtypeStruct((t, 2048), BF16 if prompt else F32)]
    out_specs = [row(2048)]
    if prompt:
        out_shape += [jax.ShapeDtypeStruct((t // tm, 512, tm), BF16)] * 2
        out_specs += [pl.BlockSpec((1, 512, tm), lambda i: (i, 0, 0))] * 2
    out_shape += [jax.ShapeDtypeStruct((t, 512), F32)] * 6 + [jax.ShapeDtypeStruct((t, 1024), F32)] * 2
    out_specs += [row(512)] * 6 + [row(1024)] * 2
    return pl.pallas_call(
        functools.partial(_inproj_kernel, prompt=prompt),
        grid=(t // tm,),
        in_specs=[row(D_MODEL),
                  pl.BlockSpec((1, D_MODEL), lambda i: (0, 0)),
                  pl.BlockSpec((D_MODEL, d_in), lambda i: (0, 0))],
        out_specs=out_specs,
        out_shape=out_shape,
        compiler_params=pltpu.CompilerParams(
            dimension_semantics=("arbitrary",), vmem_limit_bytes=VMEM_LIMIT),
        name="inproj",
    )(x, g, w_bf16)


def _load_queries_t(q_ref, qt_ref):
    qt = q_ref[...].astype(F32).T
    d = lax.broadcasted_iota(jnp.int32, qt.shape, 0)
    qt_ref[:, 0:Q_BLK] = jnp.where(d < HALF, qt, 0.0).astype(BF16)
    qt_ref[:, Q_BLK:2 * Q_BLK] = jnp.where(d >= HALF, qt, 0.0).astype(BF16)


def _key_query_positions(kb, i):
    kpos = kb * K_BLK + lax.broadcasted_iota(jnp.int32, (K_BLK, 2 * Q_BLK), 0)
    qpos = i * Q_BLK + (lax.broadcasted_iota(jnp.int32, (K_BLK, 2 * Q_BLK), 1) & (Q_BLK - 1))
    return kpos, qpos


def _sb_prompt_kernel(q_ref, k_ref, vt_ref, z_ref, tri_ref, o_ref, qt_ref, acc_ref, car_ref):
    i = pl.program_id(1)
    _load_queries_t(q_ref, qt_ref)
    acc_ref[...] = jnp.zeros_like(acc_ref)
    car_ref[...] = jnp.zeros_like(car_ref)

    def group(kb_right, n, masked):
        car = car_ref[...]
        ws = []
        for kb in [kb_right - b for b in range(n)]:
            k = k_ref[pl.ds(pl.multiple_of(kb * K_BLK, K_BLK), K_BLK), :]
            z = jnp.dot(k, qt_ref[...], preferred_element_type=F32)
            nlk = _softplus(z)
            if masked:
                kpos, qpos = _key_query_positions(kb, i)
                mask = kpos < qpos
                nlk = jnp.where(mask, nlk, 0.0)
            hi, lo = _split_bf16(nlk)
            incl = jnp.dot(tri_ref[...], jnp.concatenate([hi, lo], axis=0), preferred_element_type=F32)
            w = jnp.exp(z + incl + car)
            if masked:
                w = jnp.where(mask, w, 0.0)
            ws.append(w.astype(BF16))
            car = car - jnp.sum(nlk, axis=0, keepdims=True)
        car_ref[...] = car
        vt = jnp.concatenate([vt_ref[kb_right - b] for b in range(n)], axis=1)
        acc_ref[...] += jnp.dot(vt, jnp.concatenate(ws, axis=0), preferred_element_type=F32)

    group(2 * i + 1, 2, True)

    def body(j, carry):
        group(2 * i - 1 - FAR_GROUP * j, FAR_GROUP, False)
        return carry

    n_far = 2 * i
    lax.fori_loop(0, n_far // FAR_GROUP, body, 0)

    @pl.when(n_far % FAR_GROUP == 2)
    def _():
        group(1, 2, False)

    row = lax.broadcasted_iota(jnp.int32, (LANES, Q_BLK), 0)
    ot = jnp.where(row < HALF, acc_ref[:, 0:Q_BLK], acc_ref[:, Q_BLK:2 * Q_BLK])
    o_ref[...] = (ot.T * _silu(z_ref[...])).astype(o_ref.dtype)


def _sb_prompt(qk, vat, za, tri):
    t = qk.shape[0]
    nkb = vat.shape[0]
    return pl.pallas_call(
        _sb_prompt_kernel,
        grid=(N_GROUPS, t // Q_BLK),
        in_specs=[pl.BlockSpec((Q_BLK, LANES), lambda g, i: (i, QA0 + g)),
                  pl.BlockSpec((t, LANES), lambda g, i: (0, KA0 + g)),
                  pl.BlockSpec((nkb, LANES, K_BLK), lambda g, i: (0, g, 0)),
                  pl.BlockSpec((Q_BLK, LANES), lambda g, i: (i, g)),
                  pl.BlockSpec((K_BLK, 2 * K_BLK), lambda g, i: (0, 0))],
        out_specs=pl.BlockSpec((Q_BLK, LANES), lambda g, i: (i, g)),
        out_shape=jax.ShapeDtypeStruct((t, W_A), BF16),
        scratch_shapes=[pltpu.VMEM((LANES, 2 * Q_BLK), BF16), pltpu.VMEM((LANES, 2 * Q_BLK), F32),
                        pltpu.VMEM((1, 2 * Q_BLK), F32)],
        compiler_params=pltpu.CompilerParams(
            dimension_semantics=("arbitrary", "arbitrary"), vmem_limit_bytes=VMEM_LIMIT),
        name="sb_prompt",
    )(qk, qk, vat, za, tri)


def _diff_prompt_kernel(sc_ref, q_ref, k_ref, vt_ref, z_ref, bias_ref, g_ref, o_ref,
                        qt_ref, acc_ref, m_ref, l_ref):
    h = pl.program_id(0)
    i = pl.program_id(1)
    _load_queries_t(q_ref, qt_ref)
    acc_ref[...] = jnp.zeros_like(acc_ref)
    l_ref[...] = jnp.zeros_like(l_ref)
    m_ref[...] = jnp.full_like(m_ref, NEG_INF)
    far = sc_ref[2 + h]

    def group(kb_right, nears, masked):
        n = len(nears)
        all_far = all(near is None for near in nears)
        ss = []
        for kb, near in zip([kb_right - b for b in range(n)], nears):
            k = k_ref[pl.ds(pl.multiple_of(kb * K_BLK, K_BLK), K_BLK), :]
            s = jnp.dot(k, qt_ref[...], preferred_element_type=F32)
            if near is not None:
                b = bias_ref[0, near]
                s = s + jnp.concatenate([b, b], axis=1)
            elif not all_far:
                s = s + far
            if masked:
                kpos, qpos = _key_query_positions(kb, i)
                s = jnp.where(kpos <= qpos, s, NEG_INF)
            ss.append(s)
        m_old = m_ref[...]
        top = jnp.max(ss[0], axis=0, keepdims=True)
        for s in ss[1:]:
            top = jnp.maximum(top, jnp.max(s, axis=0, keepdims=True))
        if all_far:
            m_new = jnp.maximum(m_old, top + far)
            shift = m_new - far
        else:
            m_new = jnp.maximum(m_old, top)
            shift = m_new
        alpha = jnp.exp(m_old - m_new)
        ps = [jnp.exp(s - shift) for s in ss]
        l_new = alpha * l_ref[...]
        for p in ps:
            l_new = l_new + jnp.sum(p, axis=0, keepdims=True)
        l_ref[...] = l_new
        vt = jnp.concatenate([vt_ref[kb_right - b] for b in range(n)], axis=1)
        pcat = jnp.concatenate([p.astype(BF16) for p in ps], axis=0)
        acc_ref[...] = alpha * acc_ref[...] + jnp.dot(vt, pcat, preferred_element_type=F32)
        m_ref[...] = m_new

    group(2 * i + 1, (0, 1), True)

    @pl.when(i > 0)
    def _():
        group(2 * i - 1, (2, None), False)

    def body(j, carry):
        group(2 * i - 3 - FAR_GROUP * j, (None,) * FAR_GROUP, False)
        return carry

    n_far = jnp.maximum(2 * i - 2, 0)
    lax.fori_loop(0, n_far // FAR_GROUP, body, 0)

    @pl.when(n_far % FAR_GROUP == 2)
    def _():
        group(1, (None, None), False)

    inv = 1.0 / l_ref[...]
    ot = (acc_ref[:, 0:Q_BLK] * inv[:, 0:Q_BLK]
          - sc_ref[0] * (acc_ref[:, Q_BLK:2 * Q_BLK] * inv[:, Q_BLK:2 * Q_BLK]))
    o = _subln(ot.T, g_ref[...], sc_ref[1])
    o_ref[...] = (o * _silu(z_ref[...])).astype(o_ref.dtype)


def _diff_prompt(scalars, qk, vbt, zb, bias_near, subln_g):
    t = qk.shape[0]
    nkb = vbt.shape[0]
    grid_spec = pltpu.PrefetchScalarGridSpec(
        num_scalar_prefetch=1,
        grid=(H_B, t // Q_BLK),
        in_specs=[pl.BlockSpec((Q_BLK, LANES), lambda h, i, sc: (i, QB0 + h)),
                  pl.BlockSpec((t, LANES), lambda h, i, sc: (0, KB0 + h)),
                  pl.BlockSpec((nkb, LANES, K_BLK), lambda h, i, sc: (0, h, 0)),
                  pl.BlockSpec((Q_BLK, LANES), lambda h, i, sc: (i, h)),
                  pl.BlockSpec((1, 3, K_BLK, Q_BLK), lambda h, i, sc: (h, 0, 0, 0)),
                  pl.BlockSpec((1, LANES), lambda h, i, sc: (0, 0))],
        out_specs=pl.BlockSpec((Q_BLK, LANES), lambda h, i, sc: (i, h)),
        scratch_shapes=[pltpu.VMEM((LANES, 2 * Q_BLK), BF16), pltpu.VMEM((LANES, 2 * Q_BLK), F32),
                        pltpu.VMEM((1, 2 * Q_BLK), F32), pltpu.VMEM((1, 2 * Q_BLK), F32)],
    )
    return pl.pallas_call(
        _diff_prompt_kernel,
        grid_spec=grid_spec,
        out_shape=jax.ShapeDtypeStruct((t, W_B), BF16),
        compiler_params=pltpu.CompilerParams(
            dimension_semantics=("arbitrary", "arbitrary"), vmem_limit_bytes=VMEM_LIMIT),
        name="diff_prompt",
    )(scalars, qk, qk, vbt, zb, bias_near, subln_g)


def _block_diag_queries(q):
    n = q.shape[0]
    qt = jnp.concatenate([q] * (DEC_ROWS // n), axis=0)
    r = lax.broadcasted_iota(jnp.int32, qt.shape, 0)
    l = lax.broadcasted_iota(jnp.int32, qt.shape, 1)
    return jnp.where((l // HALF) == (r // n), qt, 0.0)


def _fold_rows(x, lanes_per_row_group, rows_per_group):
    r = lax.broadcasted_iota(jnp.int32, x.shape, 0)
    l = lax.broadcasted_iota(jnp.int32, x.shape, 1)
    x = jnp.where((l // lanes_per_row_group) == (r // rows_per_group), x, 0.0)
    out = x[0:8]
    for a in range(1, x.shape[0] // 8):
        out = out + x[8 * a:8 * a + 8]
    return out


def _new_row_mask(n_new, inclusive):
    r = lax.broadcasted_iota(jnp.int32, (DEC_ROWS, PAGE_SIZE), 0)
    s = lax.broadcasted_iota(jnp.int32, (DEC_ROWS, PAGE_SIZE), 1)
    i = r % n_new
    return (s <= i) if inclusive else (s < i)


def _sb_decode_kernel(pt_ref, q_ref, kn_ref, vn_ref, z_ref, tri1_ref, tri2_ref, *rest):
    npg = PAGES_PER_STEP
    kt_refs, vt_refs = rest[:npg], rest[npg:2 * npg]
    o_ref, qbd_ref, acc_ref, car_ref, pad_ref = rest[2 * npg:]
    j = pl.program_id(1)
    n_new = kn_ref.shape[1]

    @pl.when(j == 0)
    def _():
        qbd_ref[...] = _block_diag_queries(q_ref[0])
        mask = _new_row_mask(n_new, inclusive=False)
        pad_ref[...] = jnp.zeros_like(pad_ref)
        pad_ref[0:n_new, :] = kn_ref[0]
        lb, lk = _log_sigmoids(_nt(qbd_ref[...], pad_ref[...]))
        lk = jnp.where(mask, lk, 0.0)
        hi, lo = _split_bf16(lk)
        tri = tri1_ref[...]
        later = jnp.dot(hi, tri, preferred_element_type=F32) + jnp.dot(lo, tri, preferred_element_type=F32)
        w = jnp.where(mask, jnp.exp(lb + later), 0.0)
        pad_ref[0:n_new, :] = vn_ref[0]
        acc_ref[...] = jnp.dot(w, pad_ref[...], preferred_element_type=F32)
        car_ref[...] = jnp.sum(lk, axis=-1, keepdims=True)

    qbd = qbd_ref[...]
    z = jnp.concatenate(
        [jnp.dot(qbd, jnp.concatenate([kt_refs[2 * a + 1][...], kt_refs[2 * a][...]], axis=1),
                 preferred_element_type=F32) for a in range(PAGE_PAIRS)], axis=0)
    lb, lk = _log_sigmoids(z)
    hi, lo = _split_bf16(lk)
    tri = tri2_ref[...]
    later = jnp.dot(hi, tri, preferred_element_type=F32) + jnp.dot(lo, tri, preferred_element_type=F32)
    rs = jnp.sum(lk, axis=-1, keepdims=True)
    car = car_ref[...]
    cars = []
    for a in range(PAGE_PAIRS):
        cars.append(car)
        car = car + rs[DEC_ROWS * a:DEC_ROWS * (a + 1)]
    car_ref[...] = car
    w = jnp.exp(lb + later + jnp.concatenate(cars, axis=0))
    contrib = acc_ref[...]
    for a in range(PAGE_PAIRS):
        vt = jnp.concatenate([vt_refs[2 * a + 1][...], vt_refs[2 * a][...]], axis=1)
        contrib = contrib + _nt(w[DEC_ROWS * a:DEC_ROWS * (a + 1)], vt)
    acc_ref[...] = contrib

    @pl.when(j == pl.num_programs(1) - 1)
    def _():
        o = _fold_rows(acc_ref[...], HALF, n_new)
        o_ref[0] = o * _silu(z_ref[0])


def _page_spec(layer, n_pages, p):
    def index_map(b, j, pt, *_):
        page = n_pages - 1 - (j * PAGES_PER_STEP + p)
        return (layer, pt[b * n_pages + page], 0, 0)
    return pl.BlockSpec((None, None, 512, PAGE_SIZE), index_map)


def _sb_decode(pt_flat, n_pages, layer, q3, kn3, vn3, za3, tri1, tri2, cache_k, cache_v):
    nb, n_new, _ = kn3.shape
    per_req = lambda w, cb: pl.BlockSpec((1, n_new, w), lambda b, j, pt: (b, 0, cb))
    whole = lambda a: pl.BlockSpec(a.shape, lambda b, j, pt: (0, 0))
    grid_spec = pltpu.PrefetchScalarGridSpec(
        num_scalar_prefetch=1,
        grid=(nb, n_pages // PAGES_PER_STEP),
        in_specs=[per_req(512, QA0 // N_GROUPS), per_req(512, 0), per_req(512, 0), per_req(512, 0),
                  whole(tri1), whole(tri2)]
                 + [_page_spec(layer, n_pages, p) for p in range(PAGES_PER_STEP)] * 2,
        out_specs=per_req(512, 0),
        scratch_shapes=[pltpu.VMEM((DEC_ROWS, 512), F32), pltpu.VMEM((DEC_ROWS, 512), F32),
                        pltpu.VMEM((DEC_ROWS, 1), F32), pltpu.VMEM((PAGE_SIZE, 512), F32)],
    )
    return pl.pallas_call(
        _sb_decode_kernel,
        grid_spec=grid_spec,
        out_shape=jax.ShapeDtypeStruct((nb, n_new, W_A), F32),
        compiler_params=pltpu.CompilerParams(
            dimension_semantics=("arbitrary", "arbitrary"), vmem_limit_bytes=VMEM_LIMIT),
        name="sb_decode",
    )(pt_flat, q3, kn3, vn3, za3, tri1, tri2,
      *([cache_k] * PAGES_PER_STEP), *([cache_v] * PAGES_PER_STEP))


def _diff_decode_kernel(pt_ref, sc_ref, q_ref, kn_ref, vn_ref, z_ref, bias_ref, g_ref, *rest):
    npg = PAGES_PER_STEP
    k_refs, v_refs = rest[:npg], rest[npg:2 * npg]
    o_ref, qbd_ref, acc_ref, m_ref, l_ref, pad_ref = rest[2 * npg:]
    j = pl.program_id(1)
    n_new = kn_ref.shape[1]
    hrows = 2 * n_new

    def head_rows(ref, h):
        return ref[pl.ds(h, PAGE_SIZE, stride=H_B), :]

    @pl.when(j == 0)
    def _():
        q = q_ref[0]
        lane = lax.broadcasted_iota(jnp.int32, (n_new, LANES), 1)
        for h in range(H_B):
            qh = q[:, LANES * h:LANES * (h + 1)]
            for c in range(2):
                qbd_ref[hrows * h + n_new * c:hrows * h + n_new * (c + 1), :] = jnp.where(
                    (lane // HALF) == c, qh, 0.0)
        pad_ref[...] = jnp.zeros_like(pad_ref)
        pad_ref[0:n_new, :] = kn_ref[0]
        s = jnp.concatenate(
            [_nt(qbd_ref[hrows * h:hrows * (h + 1), :], pad_ref[:, LANES * h:LANES * (h + 1)])
             for h in range(H_B)], axis=0)
        s = jnp.where(_new_row_mask(n_new, inclusive=True), s + bias_ref[0], NEG_INF)
        m = jnp.max(s, axis=-1, keepdims=True)
        p = jnp.exp(s - m)
        pad_ref[0:n_new, :] = vn_ref[0]
        acc_ref[...] = jnp.concatenate(
            [jnp.dot(p[hrows * h:hrows * (h + 1)], pad_ref[:, LANES * h:LANES * (h + 1)],
                     preferred_element_type=F32) for h in range(H_B)], axis=0)
        m_ref[...] = m
        l_ref[...] = jnp.sum(p, axis=-1, keepdims=True)

    far = bias_ref[2]
    far2 = jnp.concatenate([far, far], axis=1)
    near2 = jnp.concatenate([far, jnp.where(j == 0, bias_ref[1], far)], axis=1)
    blocks = []
    for a in range(PAGE_PAIRS):
        rows = []
        for h in range(H_B):
            kh = jnp.concatenate([head_rows(k_refs[2 * a + 1], h), head_rows(k_refs[2 * a], h)], axis=0)
            rows.append(_nt(qbd_ref[hrows * h:hrows * (h + 1), :], kh))
        blocks.append(jnp.concatenate(rows, axis=0) + (near2 if a == 0 else far2))
    m_old = m_ref[...]
    m_new = m_old
    for a in range(PAGE_PAIRS):
        m_new = jnp.maximum(m_new, jnp.max(blocks[a], axis=-1, keepdims=True))
    alpha = jnp.exp(m_old - m_new)
    l_new = alpha * l_ref[...]
    pv = [None] * H_B
    for a in range(PAGE_PAIRS):
        p = jnp.exp(blocks[a] - m_new)
        l_new = l_new + jnp.sum(p, axis=-1, keepdims=True)
        for h in range(H_B):
            vh = jnp.concatenate([head_rows(v_refs[2 * a + 1], h), head_rows(v_refs[2 * a], h)], axis=0)
            d = jnp.dot(p[hrows * h:hrows * (h + 1)], vh, preferred_element_type=F32)
            pv[h] = d if pv[h] is None else pv[h] + d
    acc_ref[...] = alpha * acc_ref[...] + jnp.concatenate(pv, axis=0)
    l_ref[...] = l_new
    m_ref[...] = m_new

    @pl.when(j == pl.num_programs(1) - 1)
    def _():
        o = acc_ref[...] / l_ref[...]
        g = g_ref[...]
        parts = []
        for h in range(H_B):
            oh = o[hrows * h:hrows * h + n_new] - sc_ref[0] * o[hrows * h + n_new:hrows * (h + 1)]
            parts.append(_subln(oh, g, sc_ref[1]))
        o_ref[0] = jnp.concatenate(parts, axis=-1) * _silu(z_ref[0])


def _diff_decode(pt_flat, scalars, n_pages, layer, q3, kn3, vn3, zb3, bias_dec, subln_g,
                 cache_k, cache_v):
    nb, n_new, _ = kn3.shape
    per_req = lambda w, cb: pl.BlockSpec((1, n_new, w), lambda b, j, pt, sc: (b, 0, cb))
    grid_spec = pltpu.PrefetchScalarGridSpec(
        num_scalar_prefetch=2,
        grid=(nb, n_pages // PAGES_PER_STEP),
        in_specs=[per_req(512, QB0 // N_GROUPS), per_req(512, 0), per_req(512, 0), per_req(512, 0),
                  pl.BlockSpec((3, DEC_ROWS, PAGE_SIZE), lambda b, j, pt, sc: (0, 0, 0)),
                  pl.BlockSpec((1, LANES), lambda b, j, pt, sc: (0, 0))]
                 + [_page_spec(layer, n_pages, p) for p in range(PAGES_PER_STEP)] * 2,
        out_specs=per_req(512, 0),
        scratch_shapes=[pltpu.VMEM((DEC_ROWS, LANES), F32), pltpu.VMEM((DEC_ROWS, LANES), F32),
                        pltpu.VMEM((DEC_ROWS, 1), F32), pltpu.VMEM((DEC_ROWS, 1), F32),
                        pltpu.VMEM((PAGE_SIZE, 512), F32)],
    )
    return pl.pallas_call(
        _diff_decode_kernel,
        grid_spec=grid_spec,
        out_shape=jax.ShapeDtypeStruct((nb, n_new, W_B), F32),
        compiler_params=pltpu.CompilerParams(
            dimension_semantics=("arbitrary", "arbitrary"), vmem_limit_bytes=VMEM_LIMIT),
        name="diff_decode",
    )(pt_flat, scalars, q3, kn3, vn3, zb3, bias_dec, subln_g,
      *([cache_k] * PAGES_PER_STEP), *([cache_v] * PAGES_PER_STEP))


def _merge_kernel(x_ref, oa_ref, ob_ref, ga_ref, gb_ref, wa_ref, wb_ref, wo_ref, fg_ref, y_ref,
                  *, final):
    a = jnp.dot(oa_ref[...].astype(BF16), wa_ref[...], preferred_element_type=F32)
    b = jnp.dot(ob_ref[...].astype(BF16), wb_ref[...], preferred_element_type=F32)
    merged = _sigmoid(ga_ref[...]) * a + _sigmoid(gb_ref[...]) * b
    y = x_ref[...] + jnp.dot(merged.astype(BF16), wo_ref[...], preferred_element_type=F32)
    if final:
        ms = jnp.mean(y * y, axis=-1, keepdims=True)
        y = (y * lax.rsqrt(ms + RMS_EPS)) * fg_ref[...]
    y_ref[...] = y


def _merge(x, oa, ob, ga, gb, wa, wb, wo, fg, final, tm):
    t = x.shape[0]
    row = lambda w: pl.BlockSpec((tm, w), lambda i: (i, 0))
    whole = lambda a: pl.BlockSpec(a.shape, lambda i: (0, 0))
    return pl.pallas_call(
        functools.partial(_merge_kernel, final=final),
        grid=(t // tm,),
        in_specs=[row(D_MODEL), row(W_A), row(W_B), row(D_MODEL), row(D_MODEL),
                  whole(wa), whole(wb), whole(wo), whole(fg)],
        out_specs=row(D_MODEL),
        out_shape=jax.ShapeDtypeStruct((t, D_MODEL), F32),
        compiler_params=pltpu.CompilerParams(
            dimension_semantics=("arbitrary",), vmem_limit_bytes=VMEM_LIMIT),
        name="merge",
    )(x, oa, ob, ga, gb, wa, wb, wo, fg)


def _bias_by_distance(rel_bias, n):
    d = jnp.arange(n)
    nf = jnp.maximum(d, 1).astype(F32)
    large = MAX_EXACT + (jnp.log(nf / MAX_EXACT) / math.log(MAX_DISTANCE / MAX_EXACT)
                         * (N_BUCKETS - MAX_EXACT)).astype(jnp.int32)
    large = jnp.minimum(large, N_BUCKETS - 1)
    return rel_bias[jnp.where(d < MAX_EXACT, d, large)].astype(F32)


def _strict_tri(n, upper):
    a = lax.broadcasted_iota(jnp.int32, (n, n), 0)
    b = lax.broadcasted_iota(jnp.int32, (n, n), 1)
    return ((b > a) if upper else (a > b)).astype(BF16)


def kernel(x_prompt, x_sample, cache_sb_k, cache_sb_v, cache_diff_k, cache_diff_v, page_table,
           norm_g, w_in, lambda_q1, lambda_k1, lambda_q2, lambda_k2, subln_g, w_up_a, w_up_b,
           w_out, rel_bias, final_norm_g):
    depth = w_in.shape[0]
    nb_p, t_p, _ = x_prompt.shape
    nb_s, n_new, _ = x_sample.shape
    n_pages = page_table.shape[1]
    n_pool = cache_sb_k.shape[1]
    assert nb_p == 1 and t_p % Q_BLK == 0 and Q_BLK == 2 * K_BLK and K_BLK >= MAX_DISTANCE
    assert n_new * H_A == DEC_ROWS and n_new * 2 * H_B == DEC_ROWS
    assert n_pages % PAGES_PER_STEP == 0 and PAGE_SIZE >= MAX_DISTANCE

    xp = x_prompt.reshape(t_p, D_MODEL)
    xs = x_sample.reshape(nb_s * n_new, D_MODEL)
    pt_flat = page_table.reshape(-1).astype(jnp.int32)
    caches = [jnp.transpose(c, (0, 1, 3, 4, 2)).reshape(depth, n_pool, 512, PAGE_SIZE)
              for c in (cache_sb_k, cache_sb_v)]
    caches += [c.reshape(depth, n_pool, 512, PAGE_SIZE) for c in (cache_diff_k, cache_diff_v)]
    ks = lax.broadcasted_iota(jnp.int32, (K_BLK, K_BLK), 0)
    kj = lax.broadcasted_iota(jnp.int32, (K_BLK, K_BLK), 1)
    neg_incl = -(kj >= ks).astype(BF16)
    tri_p = jnp.concatenate([neg_incl, neg_incl], axis=1)
    tri_1 = _strict_tri(PAGE_SIZE, upper=False)
    tri_2 = _strict_tri(2 * PAGE_SIZE, upper=False)
    fg = final_norm_g.reshape(1, D_MODEL).astype(F32)

    n_tab = Q_BLK + K_BLK
    tab = _bias_by_distance(rel_bias, n_tab + 1)
    far = tab[-1]
    s_k = jnp.arange(K_BLK)[:, None]
    t_q = jnp.arange(Q_BLK)[None, :]
    near = jnp.stack([tab[jnp.clip(t_q - s_k + off, 0, n_tab)] for off in (-K_BLK, 0, K_BLK)])
    bias_near = jnp.transpose(near, (3, 0, 1, 2))
    rr = jnp.arange(DEC_ROWS)
    qi = rr % n_new
    hh = rr // (2 * n_new)
    ss = jnp.arange(PAGE_SIZE)
    bias_new = tab[jnp.maximum(qi[:, None] - ss[None, :], 0), hh[:, None]]
    bias_last = tab[PAGE_SIZE + qi[:, None] - ss[None, :], hh[:, None]]
    bias_far = jnp.broadcast_to(far[hh][:, None], (DEC_ROWS, PAGE_SIZE))
    bias_dec = jnp.stack([bias_new, bias_last, bias_far]).astype(F32)

    rows_p = ([], [], [], [])
    rows_s = ([], [], [], [])
    for i in range(depth):
        lam_init = 0.8 - 0.6 * math.exp(-0.3 * i)
        lam = (jnp.exp(jnp.sum(lambda_q1[i].astype(F32) * lambda_k1[i].astype(F32)))
               - jnp.exp(jnp.sum(lambda_q2[i].astype(F32) * lambda_k2[i].astype(F32)))
               + lam_init)
        scalars = jnp.concatenate([jnp.stack([lam, jnp.asarray(1.0 - lam_init, F32)]), far]).astype(F32)
        g_in = norm_g[i].reshape(1, D_MODEL).astype(F32)
        w_bf = w_in[i].astype(BF16)
        wa, wb, wo = w_up_a[i].astype(BF16), w_up_b[i].astype(BF16), w_out[i].astype(BF16)
        sg = subln_g[i].reshape(1, LANES).astype(F32)
        final = i == depth - 1

        qk, vat, vbt, ka, va, kb, vb, za, zb, ga, gb = _inproj(xp, g_in, w_bf, True, K_BLK)
        oa = _sb_prompt(qk, vat, za, tri_p)
        ob = _diff_prompt(scalars, qk, vbt, zb, bias_near, sg)
        xp = _merge(xp, oa, ob, ga, gb, wa, wb, wo, fg, final, 512)
        for lst, a, hn in zip(rows_p, (ka, va, kb, vb), (H_A, H_A, H_B, H_B)):
            lst.append(a.reshape(nb_p, t_p, hn, 512 // hn))

        qk, ka, va, kb, vb, za, zb, ga, gb = _inproj(xs, g_in, w_bf, False, nb_s * n_new)
        r3 = lambda a: a.reshape(nb_s, n_new, a.shape[-1])
        oa = _sb_decode(pt_flat, n_pages, i, r3(qk), r3(ka), r3(va), r3(za), tri_1, tri_2,
                        caches[0], caches[1])
        ob = _diff_decode(pt_flat, scalars, n_pages, i, r3(qk), r3(kb), r3(vb), r3(zb), bias_dec, sg,
                          caches[2], caches[3])
        xs = _merge(xs, oa.reshape(-1, W_A), ob.reshape(-1, W_B), ga, gb, wa, wb, wo, fg, final,
                    nb_s * n_new)
        for lst, a, hn in zip(rows_s, (ka, va, kb, vb), (H_A, H_A, H_B, H_B)):
            lst.append(a.reshape(nb_s, n_new, hn, 512 // hn))

    y_prompt = xp.reshape(nb_p, t_p, D_MODEL)
    y_sample = xs.reshape(nb_s, n_new, D_MODEL)
    return (y_prompt, y_sample,
            jnp.stack(rows_p[0]), jnp.stack(rows_p[1]), jnp.stack(rows_p[2]), jnp.stack(rows_p[3]),
            jnp.stack(rows_s[0]), jnp.stack(rows_s[1]), jnp.stack(rows_s[2]), jnp.stack(rows_s[3]))
```

```python
import functools
import math

import jax
import jax.numpy as jnp
from jax import lax
from jax.experimental import pallas as pl
from jax.experimental.pallas import tpu as pltpu

F32 = jnp.float32
BF16 = jnp.bfloat16

D_MODEL = 1024
H_A, HD_A = 8, 64
H_B, HD_B = 4, 64
W_A = H_A * HD_A
W_B = H_B * 2 * HD_B
N_BUCKETS, MAX_EXACT, MAX_DISTANCE = 32, 16, 128
RMS_EPS, SUBLN_EPS = 1e-6, 1e-5
NEG_INF = -1e30
PAGE_SIZE = 128
QK_SCALE = 0.125

LANES = 128
HALF = 64
VMEM_LIMIT = 48 * 1024 * 1024

QA0, KA0, QB0, KB0 = 0, 4, 8, 12
N_GROUPS = W_A // LANES

Q_BLK = 512
K_BLK = 256
FAR_GROUP = 4
PAGES_PER_STEP = 16
PAGE_PAIRS = PAGES_PER_STEP // 2
DEC_ROWS = 64


def _silu(z):
    return z / (1.0 + jnp.exp(-z))


def _sigmoid(z):
    return 1.0 / (1.0 + jnp.exp(-z))


def _softplus(z):
    neg_abs = lax.bitcast_convert_type(
        lax.bitcast_convert_type(z, jnp.uint32) | jnp.uint32(0x80000000), F32)
    return jnp.maximum(z, 0.0) + jnp.log(1.0 + jnp.exp(neg_abs))


def _log_sigmoids(z):
    sp = jnp.log(1.0 + jnp.exp(-jnp.abs(z)))
    lb = jnp.minimum(z, 0.0) - sp
    return lb, lb - z


def _split_bf16(x):
    hi = x.astype(BF16)
    return hi, (x - hi.astype(F32)).astype(BF16)


def _nt(a, b):
    return lax.dot_general(a, b, (((1,), (1,)), ((), ())), preferred_element_type=F32)


def _subln(o, g, scale):
    ms = jnp.mean(o * o, axis=-1, keepdims=True)
    return ((o * lax.rsqrt(ms + SUBLN_EPS)) * g) * scale


def _inproj_kernel(x_ref, g_ref, w_ref, *outs, prompt):
    if prompt:
        qk_ref, vat_ref, vbt_ref, ka_ref, va_ref, kb_ref, vb_ref, za_ref, zb_ref, ga_ref, gb_ref = outs
    else:
        qk_ref, ka_ref, va_ref, kb_ref, vb_ref, za_ref, zb_ref, ga_ref, gb_ref = outs
    x = x_ref[...]
    ms = jnp.mean(x * x, axis=-1, keepdims=True)
    h = ((x * lax.rsqrt(ms + RMS_EPS)) * g_ref[...]).astype(BF16)

    def mm(c0, width):
        return jnp.dot(h, w_ref[:, c0:c0 + width], preferred_element_type=F32)

    qdt = qk_ref.dtype
    qk_ref[:, 0:512] = (mm(0, 512) * QK_SCALE).astype(qdt)
    ka = mm(512, 512)
    ka_ref[...] = ka
    qk_ref[:, 512:1024] = ka.astype(qdt)
    va = mm(1024, 512)
    va_ref[...] = va
    za_ref[...] = mm(1536, 512)
    qk_ref[:, 1024:1536] = (mm(2048, 512) * QK_SCALE).astype(qdt)
    kb = mm(2560, 512)
    kb_ref[...] = kb
    qk_ref[:, 1536:2048] = kb.astype(qdt)
    vb = mm(3072, 512)
    vb_ref[...] = vb
    zb_ref[...] = mm(3584, 512)
    ga_ref[...] = mm(4096, 1024)
    gb_ref[...] = mm(5120, 1024)
    if prompt:
        vat_ref[0] = va.T.astype(BF16)
        vbt_ref[0] = vb.T.astype(BF16)


def _inproj(x, g, w_bf16, prompt, tm):
    t = x.shape[0]
    d_in = w_bf16.shape[1]
    row = lambda w: pl.BlockSpec((tm, w), lambda i: (i, 0))
    out_shape = [jax.ShapeDtypeStruct((t, 2048), BF16 if prompt else F32)]
    out_specs = [row(2048)]
    if prompt:
        out_shape += [jax.ShapeDtypeStruct((t // tm, 512, tm), BF16)] * 2
        out_specs += [pl.BlockSpec((1, 512, tm), lambda i: (i, 0, 0))] * 2
    out_shape += [jax.ShapeDtypeStruct((t, 512), F32)] * 6 + [jax.ShapeDtypeStruct((t, 1024), F32)] * 2
    out_specs += [row(512)] * 6 + [row(1024)] * 2
    return pl.pallas_call(
        functools.partial(_inproj_kernel, prompt=prompt),
        grid=(t // tm,),
        in_specs=[row(D_MODEL),
                  pl.BlockSpec((1, D_MODEL), lambda i: (0, 0)),
                  pl.BlockSpec((D_MODEL, d_in), lambda i: (0, 0))],
        out_specs=out_specs,
        out_shape=out_shape,
        compiler_params=pltpu.CompilerParams(
            dimension_semantics=("arbitrary",), vmem_limit_bytes=VMEM_LIMIT),
        name="inproj",
    )(x, g, w_bf16)


def _load_queries_t(q_ref, qt_ref):
    qt = q_ref[...].astype(F32).T
    d = lax.broadcasted_iota(jnp.int32, qt.shape, 0)
    qt_ref[:, 0:Q_BLK] = jnp.where(d < HALF, qt, 0.0).astype(BF16)
    qt_ref[:, Q_BLK:2 * Q_BLK] = jnp.where(d >= HALF, qt, 0.0).astype(BF16)


def _key_query_positions(kb, i):
    kpos = kb * K_BLK + lax.broadcasted_iota(jnp.int32, (K_BLK, 2 * Q_BLK), 0)
    qpos = i * Q_BLK + (lax.broadcasted_iota(jnp.int32, (K_BLK, 2 * Q_BLK), 1) & (Q_BLK - 1))
    return kpos, qpos


def _sb_prompt_kernel(q_ref, k_ref, vt_ref, z_ref, tri_ref, o_ref, qt_ref, acc_ref, car_ref,
                      sa_ref, sb_ref):
    i = pl.program_id(1)
    _load_queries_t(q_ref, qt_ref)
    acc_ref[...] = jnp.zeros_like(acc_ref)
    car_ref[...] = jnp.zeros_like(car_ref)

    def raw_scores(kb_right):
        return tuple(
            jnp.dot(k_ref[pl.ds(pl.multiple_of((kb_right - b) * K_BLK, K_BLK), K_BLK), :], qt_ref[...],
                    preferred_element_type=F32) for b in range(2))

    def pair(kb_right, masked, raw=None):
        n = 2
        if raw is None:
            raw = raw_scores(kb_right)
        car = car_ref[...]
        ws = []
        for kb, z in zip([kb_right - b for b in range(n)], raw):
            nlk = _softplus(z)
            if masked:
                kpos, qpos = _key_query_positions(kb, i)
                mask = kpos < qpos
                nlk = jnp.where(mask, nlk, 0.0)
            incl = jnp.dot(tri_ref[...], nlk.astype(BF16), preferred_element_type=F32)
            w = jnp.exp(z + incl + car)
            if masked:
                w = jnp.where(mask, w, 0.0)
            ws.append(w.astype(BF16))
            car = car - jnp.sum(nlk, axis=0, keepdims=True)
        car_ref[...] = car
        vt = jnp.concatenate([vt_ref[kb_right - b] for b in range(n)], axis=1)
        acc_ref[...] += jnp.dot(vt, jnp.concatenate(ws, axis=0), preferred_element_type=F32)

    pair(2 * i + 1, True)

    def score_ahead(buf_ref, p):
        for b, s in enumerate(raw_scores(jnp.maximum(2 * i - 1 - 2 * p, 1))):
            buf_ref[K_BLK * b:K_BLK * (b + 1), :] = s

    def update_from(buf_ref, p):
        pair(2 * i - 1 - 2 * p, False, tuple(buf_ref[K_BLK * b:K_BLK * (b + 1), :] for b in range(2)))

    score_ahead(sa_ref, 0)

    def body(j, carry):
        score_ahead(sb_ref, 2 * j + 1)
        update_from(sa_ref, 2 * j)
        score_ahead(sa_ref, 2 * j + 2)
        update_from(sb_ref, 2 * j + 1)
        return carry

    lax.fori_loop(0, i // 2, body, 0)

    @pl.when(i % 2 == 1)
    def _():
        update_from(sa_ref, i - 1)

    row = lax.broadcasted_iota(jnp.int32, (LANES, Q_BLK), 0)
    ot = jnp.where(row < HALF, acc_ref[:, 0:Q_BLK], acc_ref[:, Q_BLK:2 * Q_BLK])
    o_ref[...] = (ot.T * _silu(z_ref[...])).astype(o_ref.dtype)


def _sb_prompt(qk, vat, za, tri):
    t = qk.shape[0]
    nkb = vat.shape[0]
    return pl.pallas_call(
        _sb_prompt_kernel,
        grid=(N_GROUPS, t // Q_BLK),
        in_specs=[pl.BlockSpec((Q_BLK, LANES), lambda g, i: (i, QA0 + g)),
                  pl.BlockSpec((t, LANES), lambda g, i: (0, KA0 + g)),
                  pl.BlockSpec((nkb, LANES, K_BLK), lambda g, i: (0, g, 0)),
                  pl.BlockSpec((Q_BLK, LANES), lambda g, i: (i, g)),
                  pl.BlockSpec((K_BLK, K_BLK), lambda g, i: (0, 0))],
        out_specs=pl.BlockSpec((Q_BLK, LANES), lambda g, i: (i, g)),
        out_shape=jax.ShapeDtypeStruct((t, W_A), BF16),
        scratch_shapes=[pltpu.VMEM((LANES, 2 * Q_BLK), BF16), pltpu.VMEM((LANES, 2 * Q_BLK), F32),
                        pltpu.VMEM((1, 2 * Q_BLK), F32),
                        pltpu.VMEM((2 * K_BLK, 2 * Q_BLK), F32), pltpu.VMEM((2 * K_BLK, 2 * Q_BLK), F32)],
        compiler_params=pltpu.CompilerParams(
            dimension_semantics=("arbitrary", "arbitrary"), vmem_limit_bytes=VMEM_LIMIT),
        name="sb_prompt",
    )(qk, qk, vat, za, tri)


def _diff_prompt_kernel(sc_ref, q_ref, k_ref, vt_ref, z_ref, bias_ref, g_ref, o_ref,
                        qt_ref, acc_ref, m_ref, l_ref, sa_ref, sb_ref):
    h = pl.program_id(0)
    i = pl.program_id(1)
    _load_queries_t(q_ref, qt_ref)
    acc_ref[...] = jnp.zeros_like(acc_ref)
    l_ref[...] = jnp.zeros_like(l_ref)
    m_ref[...] = jnp.full_like(m_ref, NEG_INF)
    far = sc_ref[2 + h]

    def raw_scores(kb_right, n):
        return tuple(
            jnp.dot(k_ref[pl.ds(pl.multiple_of((kb_right - b) * K_BLK, K_BLK), K_BLK), :], qt_ref[...],
                    preferred_element_type=F32) for b in range(n))

    def group(kb_right, nears, masked, raw=None):
        n = len(nears)
        all_far = all(near is None for near in nears)
        if raw is None:
            raw = raw_scores(kb_right, n)
        ss = []
        for kb, near, s in zip([kb_right - b for b in range(n)], nears, raw):
            if near is not None:
                b = bias_ref[0, near]
                s = s + jnp.concatenate([b, b], axis=1)
            elif not all_far:
                s = s + far
            if masked:
                kpos, qpos = _key_query_positions(kb, i)
                s = jnp.where(kpos <= qpos, s, NEG_INF)
            ss.append(s)
        m_old = m_ref[...]
        top = jnp.max(ss[0], axis=0, keepdims=True)
        for s in ss[1:]:
            top = jnp.maximum(top, jnp.max(s, axis=0, keepdims=True))
        if all_far:
            m_new = jnp.maximum(m_old, top + far)
            shift = m_new - far
        else:
            m_new = jnp.maximum(m_old, top)
            shift = m_new
        alpha = jnp.exp(m_old - m_new)
        ps = [jnp.exp(s - shift) for s in ss]
        l_new = alpha * l_ref[...]
        for p in ps:
            l_new = l_new + jnp.sum(p, axis=0, keepdims=True)
        l_ref[...] = l_new
        vt = jnp.concatenate([vt_ref[kb_right - b] for b in range(n)], axis=1)
        pcat = jnp.concatenate([p.astype(BF16) for p in ps], axis=0)
        acc_ref[...] = alpha * acc_ref[...] + jnp.dot(vt, pcat, preferred_element_type=F32)
        m_ref[...] = m_new

    group(2 * i + 1, (0, 1), True)

    @pl.when(i > 0)
    def _():
        group(2 * i - 1, (2, None), False)

    n_pairs = jnp.maximum(i - 1, 0)

    def score_ahead(buf_ref, p):
        kb_right = jnp.maximum(2 * i - 3 - 2 * p, 1)
        for b, s in enumerate(raw_scores(kb_right, 2)):
            buf_ref[K_BLK * b:K_BLK * (b + 1), :] = s

    def update_from(buf_ref, p):
        raw = tuple(buf_ref[K_BLK * b:K_BLK * (b + 1), :] for b in range(2))
        group(2 * i - 3 - 2 * p, (None, None), False, raw)

    score_ahead(sa_ref, 0)

    def body(j, carry):
        score_ahead(sb_ref, 2 * j + 1)
        update_from(sa_ref, 2 * j)
        score_ahead(sa_ref, 2 * j + 2)
        update_from(sb_ref, 2 * j + 1)
        return carry

    lax.fori_loop(0, n_pairs // 2, body, 0)

    @pl.when(n_pairs % 2 == 1)
    def _():
        update_from(sa_ref, n_pairs - 1)

    inv = 1.0 / l_ref[...]
    ot = (acc_ref[:, 0:Q_BLK] * inv[:, 0:Q_BLK]
          - sc_ref[0] * (acc_ref[:, Q_BLK:2 * Q_BLK] * inv[:, Q_BLK:2 * Q_BLK]))
    o = _subln(ot.T, g_ref[...], sc_ref[1])
    o_ref[...] = (o * _silu(z_ref[...])).astype(o_ref.dtype)


def _diff_prompt(scalars, qk, vbt, zb, bias_near, subln_g):
    t = qk.shape[0]
    nkb = vbt.shape[0]
    grid_spec = pltpu.PrefetchScalarGridSpec(
        num_scalar_prefetch=1,
        grid=(H_B, t // Q_BLK),
        in_specs=[pl.BlockSpec((Q_BLK, LANES), lambda h, i, sc: (i, QB0 + h)),
                  pl.BlockSpec((t, LANES), lambda h, i, sc: (0, KB0 + h)),
                  pl.BlockSpec((nkb, LANES, K_BLK), lambda h, i, sc: (0, h, 0)),
                  pl.BlockSpec((Q_BLK, LANES), lambda h, i, sc: (i, h)),
                  pl.BlockSpec((1, 3, K_BLK, Q_BLK), lambda h, i, sc: (h, 0, 0, 0)),
                  pl.BlockSpec((1, LANES), lambda h, i, sc: (0, 0))],
        out_specs=pl.BlockSpec((Q_BLK, LANES), lambda h, i, sc: (i, h)),
        scratch_shapes=[pltpu.VMEM((LANES, 2 * Q_BLK), BF16), pltpu.VMEM((LANES, 2 * Q_BLK), F32),
                        pltpu.VMEM((1, 2 * Q_BLK), F32), pltpu.VMEM((1, 2 * Q_BLK), F32),
                        pltpu.VMEM((2 * K_BLK, 2 * Q_BLK), F32), pltpu.VMEM((2 * K_BLK, 2 * Q_BLK), F32)],
    )
    return pl.pallas_call(
        _diff_prompt_kernel,
        grid_spec=grid_spec,
        out_shape=jax.ShapeDtypeStruct((t, W_B), BF16),
        compiler_params=pltpu.CompilerParams(
            dimension_semantics=("arbitrary", "arbitrary"), vmem_limit_bytes=VMEM_LIMIT),
        name="diff_prompt",
    )(scalars, qk, qk, vbt, zb, bias_near, subln_g)


def _block_diag_queries(q):
    n = q.shape[0]
    qt = jnp.concatenate([q] * (DEC_ROWS // n), axis=0)
    r = lax.broadcasted_iota(jnp.int32, qt.shape, 0)
    l = lax.broadcasted_iota(jnp.int32, qt.shape, 1)
    return jnp.where((l // HALF) == (r // n), qt, 0.0)


def _fold_rows(x, lanes_per_row_group, rows_per_group):
    r = lax.broadcasted_iota(jnp.int32, x.shape, 0)
    l = lax.broadcasted_iota(jnp.int32, x.shape, 1)
    x = jnp.where((l // lanes_per_row_group) == (r // rows_per_group), x, 0.0)
    out = x[0:8]
    for a in range(1, x.shape[0] // 8):
        out = out + x[8 * a:8 * a + 8]
    return out


def _new_row_mask(n_new, inclusive):
    r = lax.broadcasted_iota(jnp.int32, (DEC_ROWS, PAGE_SIZE), 0)
    s = lax.broadcasted_iota(jnp.int32, (DEC_ROWS, PAGE_SIZE), 1)
    i = r % n_new
    return (s <= i) if inclusive else (s < i)


def _sb_decode_kernel(pt_ref, q_ref, kn_ref, vn_ref, z_ref, tri1_ref, tri2_ref, *rest):
    npg = PAGES_PER_STEP
    kt_refs, vt_refs = rest[:npg], rest[npg:2 * npg]
    o_ref, qbd_ref, acc_ref, car_ref, pad_ref = rest[2 * npg:]
    j = pl.program_id(1)
    n_new = kn_ref.shape[1]

    @pl.when(j == 0)
    def _():
        qbd_ref[...] = _block_diag_queries(q_ref[0])
        mask = _new_row_mask(n_new, inclusive=False)
        pad_ref[...] = jnp.zeros_like(pad_ref)
        pad_ref[0:n_new, :] = kn_ref[0]
        lb, lk = _log_sigmoids(_nt(qbd_ref[...], pad_ref[...]))
        lk = jnp.where(mask, lk, 0.0)
        hi, lo = _split_bf16(lk)
        tri = tri1_ref[...]
        later = jnp.dot(hi, tri, preferred_element_type=F32) + jnp.dot(lo, tri, preferred_element_type=F32)
        w = jnp.where(mask, jnp.exp(lb + later), 0.0)
        pad_ref[0:n_new, :] = vn_ref[0]
        acc_ref[...] = jnp.dot(w, pad_ref[...], preferred_element_type=F32)
        car_ref[...] = jnp.sum(lk, axis=-1, keepdims=True)

    qbd = qbd_ref[...]
    z = jnp.concatenate(
        [jnp.dot(qbd, jnp.concatenate([kt_refs[2 * a + 1][...], kt_refs[2 * a][...]], axis=1),
                 preferred_element_type=F32) for a in range(PAGE_PAIRS)], axis=0)
    lb, lk = _log_sigmoids(z)
    hi, lo = _split_bf16(lk)
    tri = tri2_ref[...]
    later = jnp.dot(hi, tri, preferred_element_type=F32) + jnp.dot(lo, tri, preferred_element_type=F32)
    rs = jnp.sum(lk, axis=-1, keepdims=True)
    car = car_ref[...]
    cars = []
    for a in range(PAGE_PAIRS):
        cars.append(car)
        car = car + rs[DEC_ROWS * a:DEC_ROWS * (a + 1)]
    car_ref[...] = car
    w = jnp.exp(lb + later + jnp.concatenate(cars, axis=0))
    contrib = acc_ref[...]
    for a in range(PAGE_PAIRS):
        vt = jnp.concatenate([vt_refs[2 * a + 1][...], vt_refs[2 * a][...]], axis=1)
        contrib = contrib + _nt(w[DEC_ROWS * a:DEC_ROWS * (a + 1)], vt)
    acc_ref[...] = contrib

    @pl.when(j == pl.num_programs(1) - 1)
    def _():
        o = _fold_rows(acc_ref[...], HALF, n_new)
        o_ref[0] = o * _silu(z_ref[0])


def _page_spec(layer, n_pages, p):
    def index_map(b, j, pt, *_):
        page = n_pages - 1 - (j * PAGES_PER_STEP + p)
        return (layer, pt[b * n_pages + page], 0, 0)
    return pl.BlockSpec((None, None, 512, PAGE_SIZE), index_map)


def _sb_decode(pt_flat, n_pages, layer, q3, kn3, vn3, za3, tri1, tri2, cache_k, cache_v):
    nb, n_new, _ = kn3.shape
    per_req = lambda w, cb: pl.BlockSpec((1, n_new, w), lambda b, j, pt: (b, 0, cb))
    whole = lambda a: pl.BlockSpec(a.shape, lambda b, j, pt: (0, 0))
    grid_spec = pltpu.PrefetchScalarGridSpec(
        num_scalar_prefetch=1,
        grid=(nb, n_pages // PAGES_PER_STEP),
        in_specs=[per_req(512, QA0 // N_GROUPS), per_req(512, 0), per_req(512, 0), per_req(512, 0),
                  whole(tri1), whole(tri2)]
                 + [_page_spec(layer, n_pages, p) for p in range(PAGES_PER_STEP)] * 2,
        out_specs=per_req(512, 0),
        scratch_shapes=[pltpu.VMEM((DEC_ROWS, 512), F32), pltpu.VMEM((DEC_ROWS, 512), F32),
                        pltpu.VMEM((DEC_ROWS, 1), F32), pltpu.VMEM((PAGE_SIZE, 512), F32)],
    )
    return pl.pallas_call(
        _sb_decode_kernel,
        grid_spec=grid_spec,
        out_shape=jax.ShapeDtypeStruct((nb, n_new, W_A), F32),
        compiler_params=pltpu.CompilerParams(
            dimension_semantics=("arbitrary", "arbitrary"), vmem_limit_bytes=VMEM_LIMIT),
        name="sb_decode",
    )(pt_flat, q3, kn3, vn3, za3, tri1, tri2,
      *([cache_k] * PAGES_PER_STEP), *([cache_v] * PAGES_PER_STEP))


def _diff_decode_kernel(pt_ref, sc_ref, q_ref, kn_ref, vn_ref, z_ref, bias_ref, g_ref, *rest):
    npg = PAGES_PER_STEP
    k_refs, v_refs = rest[:npg], rest[npg:2 * npg]
    o_ref, qbd_ref, acc_ref, m_ref, l_ref, pad_ref = rest[2 * npg:]
    j = pl.program_id(1)
    n_new = kn_ref.shape[1]
    hrows = 2 * n_new

    def head_rows(ref, h):
        return ref[pl.ds(h, PAGE_SIZE, stride=H_B), :]

    @pl.when(j == 0)
    def _():
        q = q_ref[0]
        lane = lax.broadcasted_iota(jnp.int32, (n_new, LANES), 1)
        for h in range(H_B):
            qh = q[:, LANES * h:LANES * (h + 1)]
            for c in range(2):
                qbd_ref[hrows * h + n_new * c:hrows * h + n_new * (c + 1), :] = jnp.where(
                    (lane // HALF) == c, qh, 0.0)
        pad_ref[...] = jnp.zeros_like(pad_ref)
        pad_ref[0:n_new, :] = kn_ref[0]
        s = jnp.concatenate(
            [_nt(qbd_ref[hrows * h:hrows * (h + 1), :], pad_ref[:, LANES * h:LANES * (h + 1)])
             for h in range(H_B)], axis=0)
        s = jnp.where(_new_row_mask(n_new, inclusive=True), s + bias_ref[0], NEG_INF)
        m = jnp.max(s, axis=-1, keepdims=True)
        p = jnp.exp(s - m)
        pad_ref[0:n_new, :] = vn_ref[0]
        acc_ref[...] = jnp.concatenate(
            [jnp.dot(p[hrows * h:hrows * (h + 1)], pad_ref[:, LANES * h:LANES * (h + 1)],
                     preferred_element_type=F32) for h in range(H_B)], axis=0)
        m_ref[...] = m
        l_ref[...] = jnp.sum(p, axis=-1, keepdims=True)

    far = bias_ref[2]
    far2 = jnp.concatenate([far, far], axis=1)
    near2 = jnp.concatenate([far, jnp.where(j == 0, bias_ref[1], far)], axis=1)
    blocks = []
    for a in range(PAGE_PAIRS):
        rows = []
        for h in range(H_B):
            kh = jnp.concatenate([head_rows(k_refs[2 * a + 1], h), head_rows(k_refs[2 * a], h)], axis=0)
            rows.append(_nt(qbd_ref[hrows * h:hrows * (h + 1), :], kh))
        blocks.append(jnp.concatenate(rows, axis=0) + (near2 if a == 0 else far2))
    m_old = m_ref[...]
    m_new = m_old
    for a in range(PAGE_PAIRS):
        m_new = jnp.maximum(m_new, jnp.max(blocks[a], axis=-1, keepdims=True))
    alpha = jnp.exp(m_old - m_new)
    l_new = alpha * l_ref[...]
    pv = [None] * H_B
    for a in range(PAGE_PAIRS):
        p = jnp.exp(blocks[a] - m_new)
        l_new = l_new + jnp.sum(p, axis=-1, keepdims=True)
        for h in range(H_B):
            vh = jnp.concatenate([head_rows(v_refs[2 * a + 1], h), head_rows(v_refs[2 * a], h)], axis=0)
            d = jnp.dot(p[hrows * h:hrows * (h + 1)], vh, preferred_element_type=F32)
            pv[h] = d if pv[h] is None else pv[h] + d
    acc_ref[...] = alpha * acc_ref[...] + jnp.concatenate(pv, axis=0)
    l_ref[...] = l_new
    m_ref[...] = m_new

    @pl.when(j == pl.num_programs(1) - 1)
    def _():
        o = acc_ref[...] / l_ref[...]
        g = g_ref[...]
        parts = []
        for h in range(H_B):
            oh = o[hrows * h:hrows * h + n_new] - sc_ref[0] * o[hrows * h + n_new:hrows * (h + 1)]
            parts.append(_subln(oh, g, sc_ref[1]))
        o_ref[0] = jnp.concatenate(parts, axis=-1) * _silu(z_ref[0])


def _diff_decode(pt_flat, scalars, n_pages, layer, q3, kn3, vn3, zb3, bias_dec, subln_g,
                 cache_k, cache_v):
    nb, n_new, _ = kn3.shape
    per_req = lambda w, cb: pl.BlockSpec((1, n_new, w), lambda b, j, pt, sc: (b, 0, cb))
    grid_spec = pltpu.PrefetchScalarGridSpec(
        num_scalar_prefetch=2,
        grid=(nb, n_pages // PAGES_PER_STEP),
        in_specs=[per_req(512, QB0 // N_GROUPS), per_req(512, 0), per_req(512, 0), per_req(512, 0),
                  pl.BlockSpec((3, DEC_ROWS, PAGE_SIZE), lambda b, j, pt, sc: (0, 0, 0)),
                  pl.BlockSpec((1, LANES), lambda b, j, pt, sc: (0, 0))]
                 + [_page_spec(layer, n_pages, p) for p in range(PAGES_PER_STEP)] * 2,
        out_specs=per_req(512, 0),
        scratch_shapes=[pltpu.VMEM((DEC_ROWS, LANES), F32), pltpu.VMEM((DEC_ROWS, LANES), F32),
                        pltpu.VMEM((DEC_ROWS, 1), F32), pltpu.VMEM((DEC_ROWS, 1), F32),
                        pltpu.VMEM((PAGE_SIZE, 512), F32)],
    )
    return pl.pallas_call(
        _diff_decode_kernel,
        grid_spec=grid_spec,
        out_shape=jax.ShapeDtypeStruct((nb, n_new, W_B), F32),
        compiler_params=pltpu.CompilerParams(
            dimension_semantics=("arbitrary", "arbitrary"), vmem_limit_bytes=VMEM_LIMIT),
        name="diff_decode",
    )(pt_flat, scalars, q3, kn3, vn3, zb3, bias_dec, subln_g,
      *([cache_k] * PAGES_PER_STEP), *([cache_v] * PAGES_PER_STEP))


def _merge_kernel(x_ref, oa_ref, ob_ref, ga_ref, gb_ref, wa_ref, wb_ref, wo_ref, fg_ref, y_ref,
                  *, final):
    a = jnp.dot(oa_ref[...].astype(BF16), wa_ref[...], preferred_element_type=F32)
    b = jnp.dot(ob_ref[...].astype(BF16), wb_ref[...], preferred_element_type=F32)
    merged = _sigmoid(ga_ref[...]) * a + _sigmoid(gb_ref[...]) * b
    y = x_ref[...] + jnp.dot(merged.astype(BF16), wo_ref[...], preferred_element_type=F32)
    if final:
        ms = jnp.mean(y * y, axis=-1, keepdims=True)
        y = (y * lax.rsqrt(ms + RMS_EPS)) * fg_ref[...]
    y_ref[...] = y


def _merge(x, oa, ob, ga, gb, wa, wb, wo, fg, final, tm):
    t = x.shape[0]
    row = lambda w: pl.BlockSpec((tm, w), lambda i: (i, 0))
    whole = lambda a: pl.BlockSpec(a.shape, lambda i: (0, 0))
    return pl.pallas_call(
        functools.partial(_merge_kernel, final=final),
        grid=(t // tm,),
        in_specs=[row(D_MODEL), row(W_A), row(W_B), row(D_MODEL), row(D_MODEL),
                  whole(wa), whole(wb), whole(wo), whole(fg)],
        out_specs=row(D_MODEL),
        out_shape=jax.ShapeDtypeStruct((t, D_MODEL), F32),
        compiler_params=pltpu.CompilerParams(
            dimension_semantics=("arbitrary",), vmem_limit_bytes=VMEM_LIMIT),
        name="merge",
    )(x, oa, ob, ga, gb, wa, wb, wo, fg)


def _bias_by_distance(rel_bias, n):
    d = jnp.arange(n)
    nf = jnp.maximum(d, 1).astype(F32)
    large = MAX_EXACT + (jnp.log(nf / MAX_EXACT) / math.log(MAX_DISTANCE / MAX_EXACT)
                         * (N_BUCKETS - MAX_EXACT)).astype(jnp.int32)
    large = jnp.minimum(large, N_BUCKETS - 1)
    return rel_bias[jnp.where(d < MAX_EXACT, d, large)].astype(F32)


def _strict_tri(n, upper):
    a = lax.broadcasted_iota(jnp.int32, (n, n), 0)
    b = lax.broadcasted_iota(jnp.int32, (n, n), 1)
    return ((b > a) if upper else (a > b)).astype(BF16)


def kernel(x_prompt, x_sample, cache_sb_k, cache_sb_v, cache_diff_k, cache_diff_v, page_table,
           norm_g, w_in, lambda_q1, lambda_k1, lambda_q2, lambda_k2, subln_g, w_up_a, w_up_b,
           w_out, rel_bias, final_norm_g):
    depth = w_in.shape[0]
    nb_p, t_p, _ = x_prompt.shape
    nb_s, n_new, _ = x_sample.shape
    n_pages = page_table.shape[1]
    n_pool = cache_sb_k.shape[1]
    assert nb_p == 1 and t_p % Q_BLK == 0 and Q_BLK == 2 * K_BLK and K_BLK >= MAX_DISTANCE
    assert t_p >= FAR_GROUP * K_BLK and FAR_GROUP % 2 == 0
    assert n_new * H_A == DEC_ROWS and n_new * 2 * H_B == DEC_ROWS
    assert n_pages % PAGES_PER_STEP == 0 and PAGE_SIZE >= MAX_DISTANCE

    xp = x_prompt.reshape(t_p, D_MODEL)
    xs = x_sample.reshape(nb_s * n_new, D_MODEL)
    pt_flat = page_table.reshape(-1).astype(jnp.int32)
    caches = [jnp.transpose(c, (0, 1, 3, 4, 2)).reshape(depth, n_pool, 512, PAGE_SIZE)
              for c in (cache_sb_k, cache_sb_v)]
    caches += [c.reshape(depth, n_pool, 512, PAGE_SIZE) for c in (cache_diff_k, cache_diff_v)]
    ks = lax.broadcasted_iota(jnp.int32, (K_BLK, K_BLK), 0)
    kj = lax.broadcasted_iota(jnp.int32, (K_BLK, K_BLK), 1)
    tri_p = -(kj >= ks).astype(BF16)
    tri_1 = _strict_tri(PAGE_SIZE, upper=False)
    tri_2 = _strict_tri(2 * PAGE_SIZE, upper=False)
    fg = final_norm_g.reshape(1, D_MODEL).astype(F32)

    n_tab = Q_BLK + K_BLK
    tab = _bias_by_distance(rel_bias, n_tab + 1)
    far = tab[-1]
    ext = jnp.transpose(tab[jnp.clip(jnp.arange(n_tab + Q_BLK) - (Q_BLK - 1), 0, n_tab)])
    starts = (jnp.asarray((-K_BLK, 0, K_BLK))[:, None] + (Q_BLK - 1) - jnp.arange(K_BLK)[None, :]).reshape(-1)
    windows = lambda vec: jax.vmap(lambda st: lax.dynamic_slice(vec, (st,), (Q_BLK,)))(starts)
    bias_near = jax.vmap(windows)(ext).reshape(H_B, 3, K_BLK, Q_BLK)
    rr = jnp.arange(DEC_ROWS)
    qi = rr % n_new
    hh = rr // (2 * n_new)
    ss = jnp.arange(PAGE_SIZE)
    bias_new = tab[jnp.maximum(qi[:, None] - ss[None, :], 0), hh[:, None]]
    bias_last = tab[PAGE_SIZE + qi[:, None] - ss[None, :], hh[:, None]]
    bias_far = jnp.broadcast_to(far[hh][:, None], (DEC_ROWS, PAGE_SIZE))
    bias_dec = jnp.stack([bias_new, bias_last, bias_far]).astype(F32)

    rows_p = ([], [], [], [])
    rows_s = ([], [], [], [])
    for i in range(depth):
        lam_init = 0.8 - 0.6 * math.exp(-0.3 * i)
        lam = (jnp.exp(jnp.sum(lambda_q1[i].astype(F32) * lambda_k1[i].astype(F32)))
               - jnp.exp(jnp.sum(lambda_q2[i].astype(F32) * lambda_k2[i].astype(F32)))
               + lam_init)
        scalars = jnp.concatenate([jnp.stack([lam, jnp.asarray(1.0 - lam_init, F32)]), far]).astype(F32)
        g_in = norm_g[i].reshape(1, D_MODEL).astype(F32)
        w_bf = w_in[i].astype(BF16)
        wa, wb, wo = w_up_a[i].astype(BF16), w_up_b[i].astype(BF16), w_out[i].astype(BF16)
        sg = subln_g[i].reshape(1, LANES).astype(F32)
        final = i == depth - 1

        qk, vat, vbt, ka, va, kb, vb, za, zb, ga, gb = _inproj(xp, g_in, w_bf, True, K_BLK)
        oa = _sb_prompt(qk, vat, za, tri_p)
        ob = _diff_prompt(scalars, qk, vbt, zb, bias_near, sg)
        xp = _merge(xp, oa, ob, ga, gb, wa, wb, wo, fg, final, 512)
        for lst, a, hn in zip(rows_p, (ka, va, kb, vb), (H_A, H_A, H_B, H_B)):
            lst.append(a.reshape(nb_p, t_p, hn, 512 // hn))

        qk, ka, va, kb, vb, za, zb, ga, gb = _inproj(xs, g_in, w_bf, False, nb_s * n_new)
        r3 = lambda a: a.reshape(nb_s, n_new, a.shape[-1])
        oa = _sb_decode(pt_flat, n_pages, i, r3(qk), r3(ka), r3(va), r3(za), tri_1, tri_2,
                        caches[0], caches[1])
        ob = _diff_decode(pt_flat, scalars, n_pages, i, r3(qk), r3(kb), r3(vb), r3(zb), bias_dec, sg,
                          caches[2], caches[3])
        xs = _merge(xs, oa.reshape(-1, W_A), ob.reshape(-1, W_B), ga, gb, wa, wb, wo, fg, final,
                    nb_s * n_new)
        for lst, a, hn in zip(rows_s, (ka, va, kb, vb), (H_A, H_A, H_B, H_B)):
            lst.append(a.reshape(nb_s, n_new, hn, 512 // hn))

    y_prompt = xp.reshape(nb_p, t_p, D_MODEL)
    y_sample = xs.reshape(nb_s, n_new, D_MODEL)
    return (y_prompt, y_sample,
            jnp.stack(rows_p[0]), jnp.stack(rows_p[1]), jnp.stack(rows_p[2]), jnp.stack(rows_p[3]),
            jnp.stack(rows_s[0]), jnp.stack(rows_s[1]), jnp.stack(rows_s[2]), jnp.stack(rows_s[3]))
```

```python
import functools
import math

import jax
import jax.numpy as jnp
from jax import lax
from jax.experimental import pallas as pl
from jax.experimental.pallas import tpu as pltpu

F32 = jnp.float32
BF16 = jnp.bfloat16

D_MODEL = 1024
H_A, HD_A = 8, 64
H_B, HD_B = 4, 64
W_A = H_A * HD_A
W_B = H_B * 2 * HD_B
N_BUCKETS, MAX_EXACT, MAX_DISTANCE = 32, 16, 128
RMS_EPS, SUBLN_EPS = 1e-6, 1e-5
NEG_INF = -1e30
PAGE_SIZE = 128
QK_SCALE = 0.125

LANES = 128
HALF = 64
VMEM_LIMIT = 48 * 1024 * 1024

QA0, KA0, QB0, KB0 = 0, 4, 8, 12
N_GROUPS = W_A // LANES

Q_BLK = 512
K_BLK = 256
FAR_GROUP = 4
ONES_ROWS = 16
PAGES_PER_STEP = 16
PAGE_PAIRS = PAGES_PER_STEP // 2
DEC_ROWS = 64


def _silu(z):
    return z / (1.0 + jnp.exp(-z))


def _sigmoid(z):
    return 1.0 / (1.0 + jnp.exp(-z))


def _softplus(z):
    neg_abs = lax.bitcast_convert_type(
        lax.bitcast_convert_type(z, jnp.uint32) | jnp.uint32(0x80000000), F32)
    return jnp.maximum(z, 0.0) + jnp.log(1.0 + jnp.exp(neg_abs))


def _log_sigmoids(z):
    sp = jnp.log(1.0 + jnp.exp(-jnp.abs(z)))
    lb = jnp.minimum(z, 0.0) - sp
    return lb, lb - z


def _split_bf16(x):
    hi = x.astype(BF16)
    return hi, (x - hi.astype(F32)).astype(BF16)


def _nt(a, b):
    return lax.dot_general(a, b, (((1,), (1,)), ((), ())), preferred_element_type=F32)


def _subln(o, g, scale):
    ms = jnp.mean(o * o, axis=-1, keepdims=True)
    return ((o * lax.rsqrt(ms + SUBLN_EPS)) * g) * scale


def _inproj_kernel(x_ref, g_ref, w_ref, *outs, prompt):
    if prompt:
        qk_ref, vat_ref, vbt_ref, ka_ref, va_ref, kb_ref, vb_ref, za_ref, zb_ref, ga_ref, gb_ref = outs
    else:
        qk_ref, ka_ref, va_ref, kb_ref, vb_ref, za_ref, zb_ref, ga_ref, gb_ref = outs
    x = x_ref[...]
    ms = jnp.mean(x * x, axis=-1, keepdims=True)
    h = ((x * lax.rsqrt(ms + RMS_EPS)) * g_ref[...]).astype(BF16)

    def mm(c0, width):
        return jnp.dot(h, w_ref[:, c0:c0 + width], preferred_element_type=F32)

    qdt = qk_ref.dtype
    qk_ref[:, 0:512] = (mm(0, 512) * QK_SCALE).astype(qdt)
    ka = mm(512, 512)
    ka_ref[...] = ka
    qk_ref[:, 512:1024] = ka.astype(qdt)
    va = mm(1024, 512)
    va_ref[...] = va
    za_ref[...] = mm(1536, 512)
    qk_ref[:, 1024:1536] = (mm(2048, 512) * QK_SCALE).astype(qdt)
    kb = mm(2560, 512)
    kb_ref[...] = kb
    qk_ref[:, 1536:2048] = kb.astype(qdt)
    vb = mm(3072, 512)
    vb_ref[...] = vb
    zb_ref[...] = mm(3584, 512)
    ga_ref[...] = mm(4096, 1024)
    gb_ref[...] = mm(5120, 1024)
    if prompt:
        vat_ref[0] = va.T.astype(BF16)
        vbt_ref[0] = vb.T.astype(BF16)


def _inproj(x, g, w_bf16, prompt, tm):
    t = x.shape[0]
    d_in = w_bf16.shape[1]
    row = lambda w: pl.BlockSpec((tm, w), lambda i: (i, 0))
    out_shape = [jax.ShapeDtypeStruct((t, 2048), BF16 if prompt else F32)]
    out_specs = [row(2048)]
    if prompt:
        out_shape += [jax.ShapeDtypeStruct((t // tm, 512, tm), BF16)] * 2
        out_specs += [pl.BlockSpec((1, 512, tm), lambda i: (i, 0, 0))] * 2
    out_shape += [jax.ShapeDtypeStruct((t, 512), F32)] * 6 + [jax.ShapeDtypeStruct((t, 1024), F32)] * 2
    out_specs += [row(512)] * 6 + [row(1024)] * 2
    return pl.pallas_call(
        functools.partial(_inproj_kernel, prompt=prompt),
        grid=(t // tm,),
        in_specs=[row(D_MODEL),
                  pl.BlockSpec((1, D_MODEL), lambda i: (0, 0)),
                  pl.BlockSpec((D_MODEL, d_in), lambda i: (0, 0))],
        out_specs=out_specs,
        out_shape=out_shape,
        compiler_params=pltpu.CompilerParams(
            dimension_semantics=("arbitrary",), vmem_limit_bytes=VMEM_LIMIT),
        name="inproj",
    )(x, g, w_bf16)


def _load_queries_t(q_ref, qt_ref):
    qt = q_ref[...].astype(F32).T
    d = lax.broadcasted_iota(jnp.int32, qt.shape, 0)
    qt_ref[:, 0:Q_BLK] = jnp.where(d < HALF, qt, 0.0).astype(BF16)
    qt_ref[:, Q_BLK:2 * Q_BLK] = jnp.where(d >= HALF, qt, 0.0).astype(BF16)


def _key_query_positions(kb, i):
    kpos = kb * K_BLK + lax.broadcasted_iota(jnp.int32, (K_BLK, 2 * Q_BLK), 0)
    qpos = i * Q_BLK + (lax.broadcasted_iota(jnp.int32, (K_BLK, 2 * Q_BLK), 1) & (Q_BLK - 1))
    return kpos, qpos


def _sb_prompt_kernel(q_ref, k_ref, vt_ref, z_ref, tri_ref, o_ref, qt_ref, acc_ref, car_ref,
                      sa_ref, sb_ref):
    i = pl.program_id(1)
    _load_queries_t(q_ref, qt_ref)
    acc_ref[...] = jnp.zeros_like(acc_ref)
    car_ref[...] = jnp.zeros_like(car_ref)

    def raw_scores(kb_right):
        return tuple(
            jnp.dot(k_ref[pl.ds(pl.multiple_of((kb_right - b) * K_BLK, K_BLK), K_BLK), :], qt_ref[...],
                    preferred_element_type=F32) for b in range(2))

    def pair(state, kb_right, masked, raw=None):
        n = 2
        if raw is None:
            raw = raw_scores(kb_right)
        car, acc = state
        ws = []
        for kb, z in zip([kb_right - b for b in range(n)], raw):
            nlk = _softplus(z)
            if masked:
                kpos, qpos = _key_query_positions(kb, i)
                mask = kpos < qpos
                nlk = jnp.where(mask, nlk, 0.0)
            incl = jnp.dot(tri_ref[...], nlk.astype(BF16), preferred_element_type=F32)
            w = jnp.exp(z + incl + car)
            if masked:
                w = jnp.where(mask, w, 0.0)
            ws.append(w.astype(BF16))
            car = car + incl[0:1, :]
        vt = jnp.concatenate([vt_ref[kb_right - b] for b in range(n)], axis=1)
        return car, acc + jnp.dot(vt, jnp.concatenate(ws, axis=0), preferred_element_type=F32)

    def load_state():
        return car_ref[...], acc_ref[...]

    def store_state(state):
        car_ref[...], acc_ref[...] = state

    store_state(pair(load_state(), 2 * i + 1, True))

    def score_ahead(buf_ref, p):
        for b, s in enumerate(raw_scores(jnp.maximum(2 * i - 1 - 2 * p, 1))):
            buf_ref[K_BLK * b:K_BLK * (b + 1), :] = s

    def update_from(state, buf_ref, p):
        raw = tuple(buf_ref[K_BLK * b:K_BLK * (b + 1), :] for b in range(2))
        return pair(state, 2 * i - 1 - 2 * p, False, raw)

    score_ahead(sa_ref, 0)

    def body(j, carry):
        score_ahead(sb_ref, 2 * j + 1)
        state = update_from(load_state(), sa_ref, 2 * j)
        score_ahead(sa_ref, 2 * j + 2)
        store_state(update_from(state, sb_ref, 2 * j + 1))
        return carry

    lax.fori_loop(0, i // 2, body, 0)

    @pl.when(i % 2 == 1)
    def _():
        store_state(update_from(load_state(), sa_ref, i - 1))

    row = lax.broadcasted_iota(jnp.int32, (LANES, Q_BLK), 0)
    ot = jnp.where(row < HALF, acc_ref[:, 0:Q_BLK], acc_ref[:, Q_BLK:2 * Q_BLK])
    o_ref[...] = (ot.T * _silu(z_ref[...])).astype(o_ref.dtype)


def _sb_prompt(qk, vat, za, tri):
    t = qk.shape[0]
    nkb = vat.shape[0]
    return pl.pallas_call(
        _sb_prompt_kernel,
        grid=(N_GROUPS, t // Q_BLK),
        in_specs=[pl.BlockSpec((Q_BLK, LANES), lambda g, i: (i, QA0 + g)),
                  pl.BlockSpec((t, LANES), lambda g, i: (0, KA0 + g)),
                  pl.BlockSpec((nkb, LANES, K_BLK), lambda g, i: (0, g, 0)),
                  pl.BlockSpec((Q_BLK, LANES), lambda g, i: (i, g)),
                  pl.BlockSpec((K_BLK, K_BLK), lambda g, i: (0, 0))],
        out_specs=pl.BlockSpec((Q_BLK, LANES), lambda g, i: (i, g)),
        out_shape=jax.ShapeDtypeStruct((t, W_A), BF16),
        scratch_shapes=[pltpu.VMEM((LANES, 2 * Q_BLK), BF16), pltpu.VMEM((LANES, 2 * Q_BLK), F32),
                        pltpu.VMEM((1, 2 * Q_BLK), F32),
                        pltpu.VMEM((2 * K_BLK, 2 * Q_BLK), F32), pltpu.VMEM((2 * K_BLK, 2 * Q_BLK), F32)],
        compiler_params=pltpu.CompilerParams(
            dimension_semantics=("arbitrary", "arbitrary"), vmem_limit_bytes=VMEM_LIMIT),
        name="sb_prompt",
    )(qk, qk, vat, za, tri)


def _diff_prompt_kernel(sc_ref, q_ref, k_ref, vt_ref, z_ref, bias_ref, g_ref, o_ref,
                        qt_ref, acc_ref, m_ref, sa_ref, sb_ref):
    h = pl.program_id(0)
    i = pl.program_id(1)
    _load_queries_t(q_ref, qt_ref)
    acc_ref[...] = jnp.zeros_like(acc_ref)
    m_ref[...] = jnp.full_like(m_ref, NEG_INF)
    ones = jnp.ones((ONES_ROWS, K_BLK), BF16)
    far = sc_ref[2 + h]

    def raw_scores(kb_right, n):
        return tuple(
            jnp.dot(k_ref[pl.ds(pl.multiple_of((kb_right - b) * K_BLK, K_BLK), K_BLK), :], qt_ref[...],
                    preferred_element_type=F32) for b in range(n))

    def group(state, kb_right, nears, masked, raw=None):
        m_old, acc = state
        n = len(nears)
        all_far = all(near is None for near in nears)
        if raw is None:
            raw = raw_scores(kb_right, n)
        ss = []
        for kb, near, s in zip([kb_right - b for b in range(n)], nears, raw):
            if near is not None:
                b = bias_ref[0, near]
                s = s + jnp.concatenate([b, b], axis=1)
            elif not all_far:
                s = s + far
            if masked:
                kpos, qpos = _key_query_positions(kb, i)
                s = jnp.where(kpos <= qpos, s, NEG_INF)
            ss.append(s)
        top = jnp.max(ss[0], axis=0, keepdims=True)
        for s in ss[1:]:
            top = jnp.maximum(top, jnp.max(s, axis=0, keepdims=True))
        if all_far:
            m_new = jnp.maximum(m_old, top + far)
            shift = m_new - far
        else:
            m_new = jnp.maximum(m_old, top)
            shift = m_new
        alpha = jnp.exp(m_old - m_new)
        ps = [jnp.exp(s - shift) for s in ss]
        vt = jnp.concatenate(
            [jnp.concatenate([vt_ref[kb_right - b], ones], axis=0) for b in range(n)], axis=1)
        pcat = jnp.concatenate([p.astype(BF16) for p in ps], axis=0)
        return m_new, alpha * acc + jnp.dot(vt, pcat, preferred_element_type=F32)

    def load_state():
        return m_ref[...], acc_ref[...]

    def store_state(state):
        m_ref[...], acc_ref[...] = state

    store_state(group(load_state(), 2 * i + 1, (0, 1), True))

    @pl.when(i > 0)
    def _():
        store_state(group(load_state(), 2 * i - 1, (2, None), False))

    n_pairs = jnp.maximum(i - 1, 0)

    def score_ahead(buf_ref, p):
        kb_right = jnp.maximum(2 * i - 3 - 2 * p, 1)
        for b, s in enumerate(raw_scores(kb_right, 2)):
            buf_ref[K_BLK * b:K_BLK * (b + 1), :] = s

    def update_from(state, buf_ref, p):
        raw = tuple(buf_ref[K_BLK * b:K_BLK * (b + 1), :] for b in range(2))
        return group(state, 2 * i - 3 - 2 * p, (None, None), False, raw)

    score_ahead(sa_ref, 0)

    def body(j, carry):
        score_ahead(sb_ref, 2 * j + 1)
        state = update_from(load_state(), sa_ref, 2 * j)
        score_ahead(sa_ref, 2 * j + 2)
        store_state(update_from(state, sb_ref, 2 * j + 1))
        return carry

    lax.fori_loop(0, n_pairs // 2, body, 0)

    @pl.when(n_pairs % 2 == 1)
    def _():
        store_state(update_from(load_state(), sa_ref, n_pairs - 1))

    inv = 1.0 / acc_ref[LANES:LANES + 1, :]
    ot = (acc_ref[0:LANES, 0:Q_BLK] * inv[:, 0:Q_BLK]
          - sc_ref[0] * (acc_ref[0:LANES, Q_BLK:2 * Q_BLK] * inv[:, Q_BLK:2 * Q_BLK]))
    o = _subln(ot.T, g_ref[...], sc_ref[1])
    o_ref[...] = (o * _silu(z_ref[...])).astype(o_ref.dtype)


def _diff_prompt(scalars, qk, vbt, zb, bias_near, subln_g):
    t = qk.shape[0]
    nkb = vbt.shape[0]
    grid_spec = pltpu.PrefetchScalarGridSpec(
        num_scalar_prefetch=1,
        grid=(H_B, t // Q_BLK),
        in_specs=[pl.BlockSpec((Q_BLK, LANES), lambda h, i, sc: (i, QB0 + h)),
                  pl.BlockSpec((t, LANES), lambda h, i, sc: (0, KB0 + h)),
                  pl.BlockSpec((nkb, LANES, K_BLK), lambda h, i, sc: (0, h, 0)),
                  pl.BlockSpec((Q_BLK, LANES), lambda h, i, sc: (i, h)),
                  pl.BlockSpec((1, 3, K_BLK, Q_BLK), lambda h, i, sc: (h, 0, 0, 0)),
                  pl.BlockSpec((1, LANES), lambda h, i, sc: (0, 0))],
        out_specs=pl.BlockSpec((Q_BLK, LANES), lambda h, i, sc: (i, h)),
        scratch_shapes=[pltpu.VMEM((LANES, 2 * Q_BLK), BF16),
                        pltpu.VMEM((LANES + ONES_ROWS, 2 * Q_BLK), F32),
                        pltpu.VMEM((1, 2 * Q_BLK), F32),
                        pltpu.VMEM((2 * K_BLK, 2 * Q_BLK), F32), pltpu.VMEM((2 * K_BLK, 2 * Q_BLK), F32)],
    )
    return pl.pallas_call(
        _diff_prompt_kernel,
        grid_spec=grid_spec,
        out_shape=jax.ShapeDtypeStruct((t, W_B), BF16),
        compiler_params=pltpu.CompilerParams(
            dimension_semantics=("arbitrary", "arbitrary"), vmem_limit_bytes=VMEM_LIMIT),
        name="diff_prompt",
    )(scalars, qk, qk, vbt, zb, bias_near, subln_g)


def _block_diag_queries(q):
    n = q.shape[0]
    qt = jnp.concatenate([q] * (DEC_ROWS // n), axis=0)
    r = lax.broadcasted_iota(jnp.int32, qt.shape, 0)
    l = lax.broadcasted_iota(jnp.int32, qt.shape, 1)
    return jnp.where((l // HALF) == (r // n), qt, 0.0)


def _fold_rows(x, lanes_per_row_group, rows_per_group):
    r = lax.broadcasted_iota(jnp.int32, x.shape, 0)
    l = lax.broadcasted_iota(jnp.int32, x.shape, 1)
    x = jnp.where((l // lanes_per_row_group) == (r // rows_per_group), x, 0.0)
    out = x[0:8]
    for a in range(1, x.shape[0] // 8):
        out = out + x[8 * a:8 * a + 8]
    return out


def _new_row_mask(n_new, inclusive):
    r = lax.broadcasted_iota(jnp.int32, (DEC_ROWS, PAGE_SIZE), 0)
    s = lax.broadcasted_iota(jnp.int32, (DEC_ROWS, PAGE_SIZE), 1)
    i = r % n_new
    return (s <= i) if inclusive else (s < i)


def _sb_decode_kernel(pt_ref, q_ref, kn_ref, vn_ref, z_ref, tri1_ref, tri2_ref, *rest):
    npg = PAGES_PER_STEP
    kt_refs, vt_refs = rest[:npg], rest[npg:2 * npg]
    o_ref, qbd_ref, acc_ref, car_ref, pad_ref = rest[2 * npg:]
    j = pl.program_id(1)
    n_new = kn_ref.shape[1]

    @pl.when(j == 0)
    def _():
        qbd_ref[...] = _block_diag_queries(q_ref[0])
        mask = _new_row_mask(n_new, inclusive=False)
        pad_ref[...] = jnp.zeros_like(pad_ref)
        pad_ref[0:n_new, :] = kn_ref[0]
        lb, lk = _log_sigmoids(_nt(qbd_ref[...], pad_ref[...]))
        lk = jnp.where(mask, lk, 0.0)
        hi, lo = _split_bf16(lk)
        tri = tri1_ref[...]
        later = jnp.dot(hi, tri, preferred_element_type=F32) + jnp.dot(lo, tri, preferred_element_type=F32)
        w = jnp.where(mask, jnp.exp(lb + later), 0.0)
        pad_ref[0:n_new, :] = vn_ref[0]
        acc_ref[...] = jnp.dot(w, pad_ref[...], preferred_element_type=F32)
        car_ref[...] = jnp.sum(lk, axis=-1, keepdims=True)

    qbd = qbd_ref[...]
    z = jnp.concatenate(
        [jnp.dot(qbd, jnp.concatenate([kt_refs[2 * a + 1][...], kt_refs[2 * a][...]], axis=1),
                 preferred_element_type=F32) for a in range(PAGE_PAIRS)], axis=0)
    lb, lk = _log_sigmoids(z)
    hi, lo = _split_bf16(lk)
    tri = tri2_ref[...]
    later = jnp.dot(hi, tri, preferred_element_type=F32) + jnp.dot(lo, tri, preferred_element_type=F32)
    rs = jnp.sum(lk, axis=-1, keepdims=True)
    car = car_ref[...]
    cars = []
    for a in range(PAGE_PAIRS):
        cars.append(car)
        car = car + rs[DEC_ROWS * a:DEC_ROWS * (a + 1)]
    car_ref[...] = car
    w = jnp.exp(lb + later + jnp.concatenate(cars, axis=0))
    contrib = acc_ref[...]
    for a in range(PAGE_PAIRS):
        vt = jnp.concatenate([vt_refs[2 * a + 1][...], vt_refs[2 * a][...]], axis=1)
        contrib = contrib + _nt(w[DEC_ROWS * a:DEC_ROWS * (a + 1)], vt)
    acc_ref[...] = contrib

    @pl.when(j == pl.num_programs(1) - 1)
    def _():
        o = _fold_rows(acc_ref[...], HALF, n_new)
        o_ref[0] = o * _silu(z_ref[0])


def _page_spec(layer, n_pages, p):
    def index_map(b, j, pt, *_):
        page = n_pages - 1 - (j * PAGES_PER_STEP + p)
        return (layer, pt[b * n_pages + page], 0, 0)
    return pl.BlockSpec((None, None, 512, PAGE_SIZE), index_map)


def _sb_decode(pt_flat, n_pages, layer, q3, kn3, vn3, za3, tri1, tri2, cache_k, cache_v):
    nb, n_new, _ = kn3.shape
    per_req = lambda w, cb: pl.BlockSpec((1, n_new, w), lambda b, j, pt: (b, 0, cb))
    whole = lambda a: pl.BlockSpec(a.shape, lambda b, j, pt: (0, 0))
    grid_spec = pltpu.PrefetchScalarGridSpec(
        num_scalar_prefetch=1,
        grid=(nb, n_pages // PAGES_PER_STEP),
        in_specs=[per_req(512, QA0 // N_GROUPS), per_req(512, 0), per_req(512, 0), per_req(512, 0),
                  whole(tri1), whole(tri2)]
                 + [_page_spec(layer, n_pages, p) for p in range(PAGES_PER_STEP)] * 2,
        out_specs=per_req(512, 0),
        scratch_shapes=[pltpu.VMEM((DEC_ROWS, 512), F32), pltpu.VMEM((DEC_ROWS, 512), F32),
                        pltpu.VMEM((DEC_ROWS, 1), F32), pltpu.VMEM((PAGE_SIZE, 512), F32)],
    )
    return pl.pallas_call(
        _sb_decode_kernel,
        grid_spec=grid_spec,
        out_shape=jax.ShapeDtypeStruct((nb, n_new, W_A), F32),
        compiler_params=pltpu.CompilerParams(
            dimension_semantics=("arbitrary", "arbitrary"), vmem_limit_bytes=VMEM_LIMIT),
        name="sb_decode",
    )(pt_flat, q3, kn3, vn3, za3, tri1, tri2,
      *([cache_k] * PAGES_PER_STEP), *([cache_v] * PAGES_PER_STEP))


def _diff_decode_kernel(pt_ref, sc_ref, q_ref, kn_ref, vn_ref, z_ref, bias_ref, g_ref, *rest):
    npg = PAGES_PER_STEP
    k_refs, v_refs = rest[:npg], rest[npg:2 * npg]
    o_ref, qbd_ref, acc_ref, m_ref, l_ref, pad_ref = rest[2 * npg:]
    j = pl.program_id(1)
    n_new = kn_ref.shape[1]
    hrows = 2 * n_new

    def head_rows(ref, h):
        return ref[pl.ds(h, PAGE_SIZE, stride=H_B), :]

    @pl.when(j == 0)
    def _():
        q = q_ref[0]
        lane = lax.broadcasted_iota(jnp.int32, (n_new, LANES), 1)
        for h in range(H_B):
            qh = q[:, LANES * h:LANES * (h + 1)]
            for c in range(2):
                qbd_ref[hrows * h + n_new * c:hrows * h + n_new * (c + 1), :] = jnp.where(
                    (lane // HALF) == c, qh, 0.0)
        pad_ref[...] = jnp.zeros_like(pad_ref)
        pad_ref[0:n_new, :] = kn_ref[0]
        s = jnp.concatenate(
            [_nt(qbd_ref[hrows * h:hrows * (h + 1), :], pad_ref[:, LANES * h:LANES * (h + 1)])
             for h in range(H_B)], axis=0)
        s = jnp.where(_new_row_mask(n_new, inclusive=True), s + bias_ref[0], NEG_INF)
        m = jnp.max(s, axis=-1, keepdims=True)
        p = jnp.exp(s - m)
        pad_ref[0:n_new, :] = vn_ref[0]
        acc_ref[...] = jnp.concatenate(
            [jnp.dot(p[hrows * h:hrows * (h + 1)], pad_ref[:, LANES * h:LANES * (h + 1)],
                     preferred_element_type=F32) for h in range(H_B)], axis=0)
        m_ref[...] = m
        l_ref[...] = jnp.sum(p, axis=-1, keepdims=True)

    far = bias_ref[2]
    far2 = jnp.concatenate([far, far], axis=1)
    near2 = jnp.concatenate([far, jnp.where(j == 0, bias_ref[1], far)], axis=1)
    blocks = []
    for a in range(PAGE_PAIRS):
        rows = []
        for h in range(H_B):
            kh = jnp.concatenate([head_rows(k_refs[2 * a + 1], h), head_rows(k_refs[2 * a], h)], axis=0)
            rows.append(_nt(qbd_ref[hrows * h:hrows * (h + 1), :], kh))
        blocks.append(jnp.concatenate(rows, axis=0) + (near2 if a == 0 else far2))
    m_old = m_ref[...]
    m_new = m_old
    for a in range(PAGE_PAIRS):
        m_new = jnp.maximum(m_new, jnp.max(blocks[a], axis=-1, keepdims=True))
    alpha = jnp.exp(m_old - m_new)
    l_new = alpha * l_ref[...]
    pv = [None] * H_B
    for a in range(PAGE_PAIRS):
        p = jnp.exp(blocks[a] - m_new)
        l_new = l_new + jnp.sum(p, axis=-1, keepdims=True)
        for h in range(H_B):
            vh = jnp.concatenate([head_rows(v_refs[2 * a + 1], h), head_rows(v_refs[2 * a], h)], axis=0)
            d = jnp.dot(p[hrows * h:hrows * (h + 1)], vh, preferred_element_type=F32)
            pv[h] = d if pv[h] is None else pv[h] + d
    acc_ref[...] = alpha * acc_ref[...] + jnp.concatenate(pv, axis=0)
    l_ref[...] = l_new
    m_ref[...] = m_new

    @pl.when(j == pl.num_programs(1) - 1)
    def _():
        o = acc_ref[...] / l_ref[...]
        g = g_ref[...]
        parts = []
        for h in range(H_B):
            oh = o[hrows * h:hrows * h + n_new] - sc_ref[0] * o[hrows * h + n_new:hrows * (h + 1)]
            parts.append(_subln(oh, g, sc_ref[1]))
        o_ref[0] = jnp.concatenate(parts, axis=-1) * _silu(z_ref[0])


def _diff_decode(pt_flat, scalars, n_pages, layer, q3, kn3, vn3, zb3, bias_dec, subln_g,
                 cache_k, cache_v):
    nb, n_new, _ = kn3.shape
    per_req = lambda w, cb: pl.BlockSpec((1, n_new, w), lambda b, j, pt, sc: (b, 0, cb))
    grid_spec = pltpu.PrefetchScalarGridSpec(
        num_scalar_prefetch=2,
        grid=(nb, n_pages // PAGES_PER_STEP),
        in_specs=[per_req(512, QB0 // N_GROUPS), per_req(512, 0), per_req(512, 0), per_req(512, 0),
                  pl.BlockSpec((3, DEC_ROWS, PAGE_SIZE), lambda b, j, pt, sc: (0, 0, 0)),
                  pl.BlockSpec((1, LANES), lambda b, j, pt, sc: (0, 0))]
                 + [_page_spec(layer, n_pages, p) for p in range(PAGES_PER_STEP)] * 2,
        out_specs=per_req(512, 0),
        scratch_shapes=[pltpu.VMEM((DEC_ROWS, LANES), F32), pltpu.VMEM((DEC_ROWS, LANES), F32),
                        pltpu.VMEM((DEC_ROWS, 1), F32), pltpu.VMEM((DEC_ROWS, 1), F32),
                        pltpu.VMEM((PAGE_SIZE, 512), F32)],
    )
    return pl.pallas_call(
        _diff_decode_kernel,
        grid_spec=grid_spec,
        out_shape=jax.ShapeDtypeStruct((nb, n_new, W_B), F32),
        compiler_params=pltpu.CompilerParams(
            dimension_semantics=("arbitrary", "arbitrary"), vmem_limit_bytes=VMEM_LIMIT),
        name="diff_decode",
    )(pt_flat, scalars, q3, kn3, vn3, zb3, bias_dec, subln_g,
      *([cache_k] * PAGES_PER_STEP), *([cache_v] * PAGES_PER_STEP))


def _merge_kernel(x_ref, oa_ref, ob_ref, ga_ref, gb_ref, wa_ref, wb_ref, wo_ref, fg_ref, y_ref,
                  *, final):
    a = jnp.dot(oa_ref[...].astype(BF16), wa_ref[...], preferred_element_type=F32)
    b = jnp.dot(ob_ref[...].astype(BF16), wb_ref[...], preferred_element_type=F32)
    merged = _sigmoid(ga_ref[...]) * a + _sigmoid(gb_ref[...]) * b
    y = x_ref[...] + jnp.dot(merged.astype(BF16), wo_ref[...], preferred_element_type=F32)
    if final:
        ms = jnp.mean(y * y, axis=-1, keepdims=True)
        y = (y * lax.rsqrt(ms + RMS_EPS)) * fg_ref[...]
    y_ref[...] = y


def _merge(x, oa, ob, ga, gb, wa, wb, wo, fg, final, tm):
    t = x.shape[0]
    row = lambda w: pl.BlockSpec((tm, w), lambda i: (i, 0))
    whole = lambda a: pl.BlockSpec(a.shape, lambda i: (0, 0))
    return pl.pallas_call(
        functools.partial(_merge_kernel, final=final),
        grid=(t // tm,),
        in_specs=[row(D_MODEL), row(W_A), row(W_B), row(D_MODEL), row(D_MODEL),
                  whole(wa), whole(wb), whole(wo), whole(fg)],
        out_specs=row(D_MODEL),
        out_shape=jax.ShapeDtypeStruct((t, D_MODEL), F32),
        compiler_params=pltpu.CompilerParams(
            dimension_semantics=("arbitrary",), vmem_limit_bytes=VMEM_LIMIT),
        name="merge",
    )(x, oa, ob, ga, gb, wa, wb, wo, fg)


def _bias_by_distance(rel_bias, n):
    d = jnp.arange(n)
    nf = jnp.maximum(d, 1).astype(F32)
    large = MAX_EXACT + (jnp.log(nf / MAX_EXACT) / math.log(MAX_DISTANCE / MAX_EXACT)
                         * (N_BUCKETS - MAX_EXACT)).astype(jnp.int32)
    large = jnp.minimum(large, N_BUCKETS - 1)
    return rel_bias[jnp.where(d < MAX_EXACT, d, large)].astype(F32)


def _strict_tri(n, upper):
    a = lax.broadcasted_iota(jnp.int32, (n, n), 0)
    b = lax.broadcasted_iota(jnp.int32, (n, n), 1)
    return ((b > a) if upper else (a > b)).astype(BF16)


def kernel(x_prompt, x_sample, cache_sb_k, cache_sb_v, cache_diff_k, cache_diff_v, page_table,
           norm_g, w_in, lambda_q1, lambda_k1, lambda_q2, lambda_k2, subln_g, w_up_a, w_up_b,
           w_out, rel_bias, final_norm_g):
    depth = w_in.shape[0]
    nb_p, t_p, _ = x_prompt.shape
    nb_s, n_new, _ = x_sample.shape
    n_pages = page_table.shape[1]
    n_pool = cache_sb_k.shape[1]
    assert nb_p == 1 and t_p % Q_BLK == 0 and Q_BLK == 2 * K_BLK and K_BLK >= MAX_DISTANCE
    assert t_p >= FAR_GROUP * K_BLK and FAR_GROUP % 2 == 0
    assert n_new * H_A == DEC_ROWS and n_new * 2 * H_B == DEC_ROWS
    assert n_pages % PAGES_PER_STEP == 0 and PAGE_SIZE >= MAX_DISTANCE

    xp = x_prompt.reshape(t_p, D_MODEL)
    xs = x_sample.reshape(nb_s * n_new, D_MODEL)
    pt_flat = page_table.reshape(-1).astype(jnp.int32)
    caches = [jnp.transpose(c, (0, 1, 3, 4, 2)).reshape(depth, n_pool, 512, PAGE_SIZE)
              for c in (cache_sb_k, cache_sb_v)]
    caches += [c.reshape(depth, n_pool, 512, PAGE_SIZE) for c in (cache_diff_k, cache_diff_v)]
    ks = lax.broadcasted_iota(jnp.int32, (K_BLK, K_BLK), 0)
    kj = lax.broadcasted_iota(jnp.int32, (K_BLK, K_BLK), 1)
    tri_p = -(kj >= ks).astype(BF16)
    tri_1 = _strict_tri(PAGE_SIZE, upper=False)
    tri_2 = _strict_tri(2 * PAGE_SIZE, upper=False)
    fg = final_norm_g.reshape(1, D_MODEL).astype(F32)

    n_tab = Q_BLK + K_BLK
    tab = _bias_by_distance(rel_bias, n_tab + 1)
    far = tab[-1]
    ext = jnp.transpose(tab[jnp.clip(jnp.arange(n_tab + Q_BLK) - (Q_BLK - 1), 0, n_tab)])
    ext_len = n_tab + Q_BLK
    circ = jnp.tile(ext, (1, K_BLK + 1))[:, :K_BLK * (ext_len - 1)].reshape(H_B, K_BLK, ext_len - 1)
    bias_near = jnp.stack([circ[:, :, off + Q_BLK - 1:off + 2 * Q_BLK - 1] for off in (-K_BLK, 0, K_BLK)],
                          axis=1)
    rr = jnp.arange(DEC_ROWS)
    qi = rr % n_new
    hh = rr // (2 * n_new)
    ss = jnp.arange(PAGE_SIZE)
    bias_new = tab[jnp.maximum(qi[:, None] - ss[None, :], 0), hh[:, None]]
    bias_last = tab[PAGE_SIZE + qi[:, None] - ss[None, :], hh[:, None]]
    bias_far = jnp.broadcast_to(far[hh][:, None], (DEC_ROWS, PAGE_SIZE))
    bias_dec = jnp.stack([bias_new, bias_last, bias_far]).astype(F32)

    rows_p = ([], [], [], [])
    rows_s = ([], [], [], [])
    for i in range(depth):
        lam_init = 0.8 - 0.6 * math.exp(-0.3 * i)
        lam = (jnp.exp(jnp.sum(lambda_q1[i].astype(F32) * lambda_k1[i].astype(F32)))
               - jnp.exp(jnp.sum(lambda_q2[i].astype(F32) * lambda_k2[i].astype(F32)))
               + lam_init)
        scalars = jnp.concatenate([jnp.stack([lam, jnp.asarray(1.0 - lam_init, F32)]), far]).astype(F32)
        g_in = norm_g[i].reshape(1, D_MODEL).astype(F32)
        w_bf = w_in[i].astype(BF16)
        wa, wb, wo = w_up_a[i].astype(BF16), w_up_b[i].astype(BF16), w_out[i].astype(BF16)
        sg = subln_g[i].reshape(1, LANES).astype(F32)
        final = i == depth - 1

        qk, vat, vbt, ka, va, kb, vb, za, zb, ga, gb = _inproj(xp, g_in, w_bf, True, K_BLK)
        oa = _sb_prompt(qk, vat, za, tri_p)
        ob = _diff_prompt(scalars, qk, vbt, zb, bias_near, sg)
        xp = _merge(xp, oa, ob, ga, gb, wa, wb, wo, fg, final, 512)
        for lst, a, hn in zip(rows_p, (ka, va, kb, vb), (H_A, H_A, H_B, H_B)):
            lst.append(a.reshape(nb_p, t_p, hn, 512 // hn))

        qk, ka, va, kb, vb, za, zb, ga, gb = _inproj(xs, g_in, w_bf, False, nb_s * n_new)
        r3 = lambda a: a.reshape(nb_s, n_new, a.shape[-1])
        oa = _sb_decode(pt_flat, n_pages, i, r3(qk), r3(ka), r3(va), r3(za), tri_1, tri_2,
                        caches[0], caches[1])
        ob = _diff_decode(pt_flat, scalars, n_pages, i, r3(qk), r3(kb), r3(vb), r3(zb), bias_dec, sg,
                          caches[2], caches[3])
        xs = _merge(xs, oa.reshape(-1, W_A), ob.reshape(-1, W_B), ga, gb, wa, wb, wo, fg, final,
                    nb_s * n_new)
        for lst, a, hn in zip(rows_s, (ka, va, kb, vb), (H_A, H_A, H_B, H_B)):
            lst.append(a.reshape(nb_s, n_new, hn, 512 // hn))

    y_prompt = xp.reshape(nb_p, t_p, D_MODEL)
    y_sample = xs.reshape(nb_s, n_new, D_MODEL)
    return (y_prompt, y_sample,
            jnp.stack(rows_p[0]), jnp.stack(rows_p[1]), jnp.stack(rows_p[2]), jnp.stack(rows_p[3]),
            jnp.stack(rows_s[0]), jnp.stack(rows_s[1]), jnp.stack(rows_s[2]), jnp.stack(rows_s[3]))
```

```python
import functools
import math

import jax
import jax.numpy as jnp
from jax import lax
from jax.experimental import pallas as pl
from jax.experimental.pallas import tpu as pltpu

F32 = jnp.float32
BF16 = jnp.bfloat16

D_MODEL = 1024
H_A, HD_A = 8, 64
H_B, HD_B = 4, 64
W_A = H_A * HD_A
W_B = H_B * 2 * HD_B
N_BUCKETS, MAX_EXACT, MAX_DISTANCE = 32, 16, 128
RMS_EPS, SUBLN_EPS = 1e-6, 1e-5
NEG_INF = -1e30
PAGE_SIZE = 128
QK_SCALE = 0.125

LANES = 128
HALF = 64
VMEM_LIMIT = 48 * 1024 * 1024

QA0, KA0, QB0, KB0 = 0, 4, 8, 12
N_GROUPS = W_A // LANES

Q_BLK = 512
K_BLK = 256
FAR_GROUP = 4
ONES_ROWS = 16
PAGES_PER_STEP = 32
PAGE_PAIRS = PAGES_PER_STEP // 2
DEC_ROWS = 64


def _silu(z):
    return z / (1.0 + jnp.exp(-z))


def _sigmoid(z):
    return 1.0 / (1.0 + jnp.exp(-z))


def _softplus(z):
    neg_abs = lax.bitcast_convert_type(
        lax.bitcast_convert_type(z, jnp.uint32) | jnp.uint32(0x80000000), F32)
    return jnp.maximum(z, 0.0) + jnp.log(1.0 + jnp.exp(neg_abs))


def _softplus_bf16(z):
    zb = z.astype(BF16)
    neg_abs = lax.bitcast_convert_type(
        lax.bitcast_convert_type(zb, jnp.uint16) | jnp.uint16(0x8000), BF16)
    one = jnp.asarray(1.0, BF16)
    return jnp.maximum(zb, jnp.zeros_like(zb)) + jnp.log(one + jnp.exp(neg_abs))


def _log_sigmoids(z):
    sp = jnp.log(1.0 + jnp.exp(-jnp.abs(z)))
    lb = jnp.minimum(z, 0.0) - sp
    return lb, lb - z


def _split_bf16(x):
    hi = x.astype(BF16)
    return hi, (x - hi.astype(F32)).astype(BF16)


def _nt(a, b):
    return lax.dot_general(a, b, (((1,), (1,)), ((), ())), preferred_element_type=F32)


def _subln(o, g, scale):
    ms = jnp.mean(o * o, axis=-1, keepdims=True)
    return ((o * lax.rsqrt(ms + SUBLN_EPS)) * g) * scale


def _inproj_kernel(x_ref, g_ref, w_ref, *outs, prompt):
    if prompt:
        qk_ref, vat_ref, vbt_ref, ka_ref, va_ref, kb_ref, vb_ref, za_ref, zb_ref, ga_ref, gb_ref = outs
    else:
        qk_ref, ka_ref, va_ref, kb_ref, vb_ref, za_ref, zb_ref, ga_ref, gb_ref = outs
    x = x_ref[...]
    ms = jnp.mean(x * x, axis=-1, keepdims=True)
    h = ((x * lax.rsqrt(ms + RMS_EPS)) * g_ref[...]).astype(BF16)

    def mm(c0, width):
        return jnp.dot(h, w_ref[:, c0:c0 + width], preferred_element_type=F32)

    qdt = qk_ref.dtype
    qk_ref[:, 0:512] = (mm(0, 512) * QK_SCALE).astype(qdt)
    ka = mm(512, 512)
    ka_ref[...] = ka
    qk_ref[:, 512:1024] = ka.astype(qdt)
    va = mm(1024, 512)
    va_ref[...] = va
    za_ref[...] = mm(1536, 512)
    qk_ref[:, 1024:1536] = (mm(2048, 512) * QK_SCALE).astype(qdt)
    kb = mm(2560, 512)
    kb_ref[...] = kb
    qk_ref[:, 1536:2048] = kb.astype(qdt)
    vb = mm(3072, 512)
    vb_ref[...] = vb
    zb_ref[...] = mm(3584, 512)
    ga_ref[...] = mm(4096, 1024)
    gb_ref[...] = mm(5120, 1024)
    if prompt:
        vat_ref[0] = va.T.astype(BF16)
        vbt_ref[0] = vb.T.astype(BF16)


def _inproj(x, g, w_bf16, prompt, tm):
    t = x.shape[0]
    d_in = w_bf16.shape[1]
    row = lambda w: pl.BlockSpec((tm, w), lambda i: (i, 0))
    out_shape = [jax.ShapeDtypeStruct((t, 2048), BF16 if prompt else F32)]
    out_specs = [row(2048)]
    if prompt:
        out_shape += [jax.ShapeDtypeStruct((t // tm, 512, tm), BF16)] * 2
        out_specs += [pl.BlockSpec((1, 512, tm), lambda i: (i, 0, 0))] * 2
    out_shape += [jax.ShapeDtypeStruct((t, 512), F32)] * 6 + [jax.ShapeDtypeStruct((t, 1024), F32)] * 2
    out_specs += [row(512)] * 6 + [row(1024)] * 2
    return pl.pallas_call(
        functools.partial(_inproj_kernel, prompt=prompt),
        grid=(t // tm,),
        in_specs=[row(D_MODEL),
                  pl.BlockSpec((1, D_MODEL), lambda i: (0, 0)),
                  pl.BlockSpec((D_MODEL, d_in), lambda i: (0, 0))],
        out_specs=out_specs,
        out_shape=out_shape,
        compiler_params=pltpu.CompilerParams(
            dimension_semantics=("arbitrary",), vmem_limit_bytes=VMEM_LIMIT),
        name="inproj",
    )(x, g, w_bf16)


def _load_queries_t(q_ref, qt_ref):
    qt = q_ref[...].astype(F32).T
    d = lax.broadcasted_iota(jnp.int32, qt.shape, 0)
    qt_ref[:, 0:Q_BLK] = jnp.where(d < HALF, qt, 0.0).astype(BF16)
    qt_ref[:, Q_BLK:2 * Q_BLK] = jnp.where(d >= HALF, qt, 0.0).astype(BF16)


def _key_query_positions(kb, i):
    kpos = kb * K_BLK + lax.broadcasted_iota(jnp.int32, (K_BLK, 2 * Q_BLK), 0)
    qpos = i * Q_BLK + (lax.broadcasted_iota(jnp.int32, (K_BLK, 2 * Q_BLK), 1) & (Q_BLK - 1))
    return kpos, qpos


def _sb_prompt_kernel(q_ref, k_ref, vt_ref, z_ref, tri_ref, o_ref, qt_ref, acc_ref, car_ref,
                      sa_ref, sb_ref):
    i = pl.program_id(1)
    _load_queries_t(q_ref, qt_ref)
    acc_ref[...] = jnp.zeros_like(acc_ref)
    car_ref[...] = jnp.zeros_like(car_ref)

    def raw_scores(kb_right):
        return tuple(
            jnp.dot(k_ref[pl.ds(pl.multiple_of((kb_right - b) * K_BLK, K_BLK), K_BLK), :], qt_ref[...],
                    preferred_element_type=F32) for b in range(2))

    def pair(state, kb_right, masked, raw=None):
        n = 2
        if raw is None:
            raw = raw_scores(kb_right)
        car, acc = state
        ws = []
        for kb, z in zip([kb_right - b for b in range(n)], raw):
            if masked:
                kpos, qpos = _key_query_positions(kb, i)
                mask = kpos < qpos
                nlk = jnp.where(mask, _softplus(z), 0.0).astype(BF16)
            else:
                nlk = _softplus_bf16(z)
            incl = jnp.dot(tri_ref[...], nlk, preferred_element_type=F32)
            w = jnp.exp(z + incl + car)
            if masked:
                w = jnp.where(mask, w, 0.0)
            ws.append(w.astype(BF16))
            car = car + incl[0:1, :]
        vt = jnp.concatenate([vt_ref[kb_right - b] for b in range(n)], axis=1)
        return car, acc + jnp.dot(vt, jnp.concatenate(ws, axis=0), preferred_element_type=F32)

    def load_state():
        return car_ref[...], acc_ref[...]

    def store_state(state):
        car_ref[...], acc_ref[...] = state

    store_state(pair(load_state(), 2 * i + 1, True))

    def score_ahead(buf_ref, p):
        for b, s in enumerate(raw_scores(jnp.maximum(2 * i - 1 - 2 * p, 1))):
            buf_ref[K_BLK * b:K_BLK * (b + 1), :] = s

    def update_from(state, buf_ref, p):
        raw = tuple(buf_ref[K_BLK * b:K_BLK * (b + 1), :] for b in range(2))
        return pair(state, 2 * i - 1 - 2 * p, False, raw)

    score_ahead(sa_ref, 0)

    def body(j, carry):
        score_ahead(sb_ref, 2 * j + 1)
        state = update_from(load_state(), sa_ref, 2 * j)
        score_ahead(sa_ref, 2 * j + 2)
        store_state(update_from(state, sb_ref, 2 * j + 1))
        return carry

    lax.fori_loop(0, i // 2, body, 0)

    @pl.when(i % 2 == 1)
    def _():
        store_state(update_from(load_state(), sa_ref, i - 1))

    row = lax.broadcasted_iota(jnp.int32, (LANES, Q_BLK), 0)
    ot = jnp.where(row < HALF, acc_ref[:, 0:Q_BLK], acc_ref[:, Q_BLK:2 * Q_BLK])
    o_ref[...] = (ot.T * _silu(z_ref[...])).astype(o_ref.dtype)


def _sb_prompt(qk, vat, za, tri):
    t = qk.shape[0]
    nkb = vat.shape[0]
    return pl.pallas_call(
        _sb_prompt_kernel,
        grid=(N_GROUPS, t // Q_BLK),
        in_specs=[pl.BlockSpec((Q_BLK, LANES), lambda g, i: (i, QA0 + g)),
                  pl.BlockSpec((t, LANES), lambda g, i: (0, KA0 + g)),
                  pl.BlockSpec((nkb, LANES, K_BLK), lambda g, i: (0, g, 0)),
                  pl.BlockSpec((Q_BLK, LANES), lambda g, i: (i, g)),
                  pl.BlockSpec((K_BLK, K_BLK), lambda g, i: (0, 0))],
        out_specs=pl.BlockSpec((Q_BLK, LANES), lambda g, i: (i, g)),
        out_shape=jax.ShapeDtypeStruct((t, W_A), BF16),
        scratch_shapes=[pltpu.VMEM((LANES, 2 * Q_BLK), BF16), pltpu.VMEM((LANES, 2 * Q_BLK), F32),
                        pltpu.VMEM((1, 2 * Q_BLK), F32),
                        pltpu.VMEM((2 * K_BLK, 2 * Q_BLK), F32), pltpu.VMEM((2 * K_BLK, 2 * Q_BLK), F32)],
        compiler_params=pltpu.CompilerParams(
            dimension_semantics=("arbitrary", "arbitrary"), vmem_limit_bytes=VMEM_LIMIT),
        name="sb_prompt",
    )(qk, qk, vat, za, tri)


def _diff_prompt_kernel(sc_ref, q_ref, k_ref, vt_ref, z_ref, bias_ref, g_ref, o_ref,
                        qt_ref, acc_ref, m_ref, sa_ref, sb_ref):
    h = pl.program_id(0)
    i = pl.program_id(1)
    _load_queries_t(q_ref, qt_ref)
    acc_ref[...] = jnp.zeros_like(acc_ref)
    m_ref[...] = jnp.full_like(m_ref, NEG_INF)
    ones = jnp.ones((ONES_ROWS, K_BLK), BF16)
    far = sc_ref[2 + h]

    def raw_scores(kb_right, n):
        return tuple(
            jnp.dot(k_ref[pl.ds(pl.multiple_of((kb_right - b) * K_BLK, K_BLK), K_BLK), :], qt_ref[...],
                    preferred_element_type=F32) for b in range(n))

    def group(state, kb_right, nears, masked, raw=None):
        m_old, acc = state
        n = len(nears)
        all_far = all(near is None for near in nears)
        if raw is None:
            raw = raw_scores(kb_right, n)
        ss = []
        for kb, near, s in zip([kb_right - b for b in range(n)], nears, raw):
            if near is not None:
                b = bias_ref[0, near]
                s = s + jnp.concatenate([b, b], axis=1)
            elif not all_far:
                s = s + far
            if masked:
                kpos, qpos = _key_query_positions(kb, i)
                s = jnp.where(kpos <= qpos, s, NEG_INF)
            ss.append(s)
        top = jnp.max(ss[0], axis=0, keepdims=True)
        for s in ss[1:]:
            top = jnp.maximum(top, jnp.max(s, axis=0, keepdims=True))
        if all_far:
            m_new = jnp.maximum(m_old, top + far)
            shift = m_new - far
        else:
            m_new = jnp.maximum(m_old, top)
            shift = m_new
        alpha = jnp.exp(m_old - m_new)
        ps = [jnp.exp(s - shift) for s in ss]
        vt = jnp.concatenate(
            [jnp.concatenate([vt_ref[kb_right - b], ones], axis=0) for b in range(n)], axis=1)
        pcat = jnp.concatenate([p.astype(BF16) for p in ps], axis=0)
        return m_new, alpha * acc + jnp.dot(vt, pcat, preferred_element_type=F32)

    def load_state():
        return m_ref[...], acc_ref[...]

    def store_state(state):
        m_ref[...], acc_ref[...] = state

    store_state(group(load_state(), 2 * i + 1, (0, 1), True))

    @pl.when(i > 0)
    def _():
        store_state(group(load_state(), 2 * i - 1, (2, None), False))

    n_pairs = jnp.maximum(i - 1, 0)

    def score_ahead(buf_ref, p):
        kb_right = jnp.maximum(2 * i - 3 - 2 * p, 1)
        for b, s in enumerate(raw_scores(kb_right, 2)):
            buf_ref[K_BLK * b:K_BLK * (b + 1), :] = s

    def update_from(state, buf_ref, p):
        raw = tuple(buf_ref[K_BLK * b:K_BLK * (b + 1), :] for b in range(2))
        return group(state, 2 * i - 3 - 2 * p, (None, None), False, raw)

    score_ahead(sa_ref, 0)

    def body(j, carry):
        score_ahead(sb_ref, 2 * j + 1)
        state = update_from(load_state(), sa_ref, 2 * j)
        score_ahead(sa_ref, 2 * j + 2)
        store_state(update_from(state, sb_ref, 2 * j + 1))
        return carry

    lax.fori_loop(0, n_pairs // 2, body, 0)

    @pl.when(n_pairs % 2 == 1)
    def _():
        store_state(update_from(load_state(), sa_ref, n_pairs - 1))

    inv = 1.0 / acc_ref[LANES:LANES + 1, :]
    ot = (acc_ref[0:LANES, 0:Q_BLK] * inv[:, 0:Q_BLK]
          - sc_ref[0] * (acc_ref[0:LANES, Q_BLK:2 * Q_BLK] * inv[:, Q_BLK:2 * Q_BLK]))
    o = _subln(ot.T, g_ref[...], sc_ref[1])
    o_ref[...] = (o * _silu(z_ref[...])).astype(o_ref.dtype)


def _diff_prompt(scalars, qk, vbt, zb, bias_near, subln_g):
    t = qk.shape[0]
    nkb = vbt.shape[0]
    grid_spec = pltpu.PrefetchScalarGridSpec(
        num_scalar_prefetch=1,
        grid=(H_B, t // Q_BLK),
        in_specs=[pl.BlockSpec((Q_BLK, LANES), lambda h, i, sc: (i, QB0 + h)),
                  pl.BlockSpec((t, LANES), lambda h, i, sc: (0, KB0 + h)),
                  pl.BlockSpec((nkb, LANES, K_BLK), lambda h, i, sc: (0, h, 0)),
                  pl.BlockSpec((Q_BLK, LANES), lambda h, i, sc: (i, h)),
                  pl.BlockSpec((1, 3, K_BLK, Q_BLK), lambda h, i, sc: (h, 0, 0, 0)),
                  pl.BlockSpec((1, LANES), lambda h, i, sc: (0, 0))],
        out_specs=pl.BlockSpec((Q_BLK, LANES), lambda h, i, sc: (i, h)),
        scratch_shapes=[pltpu.VMEM((LANES, 2 * Q_BLK), BF16),
                        pltpu.VMEM((LANES + ONES_ROWS, 2 * Q_BLK), F32),
                        pltpu.VMEM((1, 2 * Q_BLK), F32),
                        pltpu.VMEM((2 * K_BLK, 2 * Q_BLK), F32), pltpu.VMEM((2 * K_BLK, 2 * Q_BLK), F32)],
    )
    return pl.pallas_call(
        _diff_prompt_kernel,
        grid_spec=grid_spec,
        out_shape=jax.ShapeDtypeStruct((t, W_B), BF16),
        compiler_params=pltpu.CompilerParams(
            dimension_semantics=("arbitrary", "arbitrary"), vmem_limit_bytes=VMEM_LIMIT),
        name="diff_prompt",
    )(scalars, qk, qk, vbt, zb, bias_near, subln_g)


def _block_diag_queries(q):
    n = q.shape[0]
    qt = jnp.concatenate([q] * (DEC_ROWS // n), axis=0)
    r = lax.broadcasted_iota(jnp.int32, qt.shape, 0)
    l = lax.broadcasted_iota(jnp.int32, qt.shape, 1)
    return jnp.where((l // HALF) == (r // n), qt, 0.0)


def _fold_rows(x, lanes_per_row_group, rows_per_group):
    r = lax.broadcasted_iota(jnp.int32, x.shape, 0)
    l = lax.broadcasted_iota(jnp.int32, x.shape, 1)
    x = jnp.where((l // lanes_per_row_group) == (r // rows_per_group), x, 0.0)
    out = x[0:8]
    for a in range(1, x.shape[0] // 8):
        out = out + x[8 * a:8 * a + 8]
    return out


def _new_row_mask(n_new, inclusive):
    r = lax.broadcasted_iota(jnp.int32, (DEC_ROWS, PAGE_SIZE), 0)
    s = lax.broadcasted_iota(jnp.int32, (DEC_ROWS, PAGE_SIZE), 1)
    i = r % n_new
    return (s <= i) if inclusive else (s < i)


def _sb_decode_kernel(pt_ref, q_ref, kn_ref, vn_ref, z_ref, tri1_ref, tri2_ref, *rest):
    npg = PAGES_PER_STEP
    kt_refs, vt_refs = rest[:npg], rest[npg:2 * npg]
    o_ref, qbd_ref, acc_ref, car_ref, pad_ref = rest[2 * npg:]
    j = pl.program_id(1)
    n_new = kn_ref.shape[1]

    @pl.when(j == 0)
    def _():
        qbd_ref[...] = _block_diag_queries(q_ref[0])
        mask = _new_row_mask(n_new, inclusive=False)
        pad_ref[...] = jnp.zeros_like(pad_ref)
        pad_ref[0:n_new, :] = kn_ref[0]
        lb, lk = _log_sigmoids(_nt(qbd_ref[...], pad_ref[...]))
        lk = jnp.where(mask, lk, 0.0)
        hi, lo = _split_bf16(lk)
        tri = tri1_ref[...]
        later = jnp.dot(hi, tri, preferred_element_type=F32) + jnp.dot(lo, tri, preferred_element_type=F32)
        w = jnp.where(mask, jnp.exp(lb + later), 0.0)
        pad_ref[0:n_new, :] = vn_ref[0]
        acc_ref[...] = jnp.dot(w, pad_ref[...], preferred_element_type=F32)
        car_ref[...] = jnp.sum(lk, axis=-1, keepdims=True)

    qbd = qbd_ref[...]
    z = jnp.concatenate(
        [jnp.dot(qbd, jnp.concatenate([kt_refs[2 * a + 1][...], kt_refs[2 * a][...]], axis=1),
                 preferred_element_type=F32) for a in range(PAGE_PAIRS)], axis=0)
    lb, lk = _log_sigmoids(z)
    hi, lo = _split_bf16(lk)
    tri = tri2_ref[...]
    later = jnp.dot(hi, tri, preferred_element_type=F32) + jnp.dot(lo, tri, preferred_element_type=F32)
    rs = jnp.sum(lk, axis=-1, keepdims=True)
    car = car_ref[...]
    cars = []
    for a in range(PAGE_PAIRS):
        cars.append(car)
        car = car + rs[DEC_ROWS * a:DEC_ROWS * (a + 1)]
    car_ref[...] = car
    w = jnp.exp(lb + later + jnp.concatenate(cars, axis=0))
    contrib = acc_ref[...]
    for a in range(PAGE_PAIRS):
        vt = jnp.concatenate([vt_refs[2 * a + 1][...], vt_refs[2 * a][...]], axis=1)
        contrib = contrib + _nt(w[DEC_ROWS * a:DEC_ROWS * (a + 1)], vt)
    acc_ref[...] = contrib

    @pl.when(j == pl.num_programs(1) - 1)
    def _():
        o = _fold_rows(acc_ref[...], HALF, n_new)
        o_ref[0] = o * _silu(z_ref[0])


def _page_spec(layer, n_pages, p):
    def index_map(b, j, pt, *_):
        page = n_pages - 1 - (j * PAGES_PER_STEP + p)
        return (layer, pt[b * n_pages + page], 0, 0)
    return pl.BlockSpec((None, None, 512, PAGE_SIZE), index_map)


def _sb_decode(pt_flat, n_pages, layer, q3, kn3, vn3, za3, tri1, tri2, cache_k, cache_v):
    nb, n_new, _ = kn3.shape
    per_req = lambda w, cb: pl.BlockSpec((1, n_new, w), lambda b, j, pt: (b, 0, cb))
    whole = lambda a: pl.BlockSpec(a.shape, lambda b, j, pt: (0, 0))
    grid_spec = pltpu.PrefetchScalarGridSpec(
        num_scalar_prefetch=1,
        grid=(nb, n_pages // PAGES_PER_STEP),
        in_specs=[per_req(512, QA0 // N_GROUPS), per_req(512, 0), per_req(512, 0), per_req(512, 0),
                  whole(tri1), whole(tri2)]
                 + [_page_spec(layer, n_pages, p) for p in range(PAGES_PER_STEP)] * 2,
        out_specs=per_req(512, 0),
        scratch_shapes=[pltpu.VMEM((DEC_ROWS, 512), F32), pltpu.VMEM((DEC_ROWS, 512), F32),
                        pltpu.VMEM((DEC_ROWS, 1), F32), pltpu.VMEM((PAGE_SIZE, 512), F32)],
    )
    return pl.pallas_call(
        _sb_decode_kernel,
        grid_spec=grid_spec,
        out_shape=jax.ShapeDtypeStruct((nb, n_new, W_A), F32),
        compiler_params=pltpu.CompilerParams(
            dimension_semantics=("arbitrary", "arbitrary"), vmem_limit_bytes=VMEM_LIMIT),
        name="sb_decode",
    )(pt_flat, q3, kn3, vn3, za3, tri1, tri2,
      *([cache_k] * PAGES_PER_STEP), *([cache_v] * PAGES_PER_STEP))


def _diff_decode_kernel(pt_ref, sc_ref, q_ref, kn_ref, vn_ref, z_ref, bias_ref, g_ref, *rest):
    npg = PAGES_PER_STEP
    k_refs, v_refs = rest[:npg], rest[npg:2 * npg]
    o_ref, qbd_ref, acc_ref, m_ref, l_ref, pad_ref = rest[2 * npg:]
    j = pl.program_id(1)
    n_new = kn_ref.shape[1]
    hrows = 2 * n_new

    def head_rows(ref, h):
        return ref[pl.ds(h, PAGE_SIZE, stride=H_B), :]

    @pl.when(j == 0)
    def _():
        q = q_ref[0]
        lane = lax.broadcasted_iota(jnp.int32, (n_new, LANES), 1)
        for h in range(H_B):
            qh = q[:, LANES * h:LANES * (h + 1)]
            for c in range(2):
                qbd_ref[hrows * h + n_new * c:hrows * h + n_new * (c + 1), :] = jnp.where(
                    (lane // HALF) == c, qh, 0.0)
        pad_ref[...] = jnp.zeros_like(pad_ref)
        pad_ref[0:n_new, :] = kn_ref[0]
        s = jnp.concatenate(
            [_nt(qbd_ref[hrows * h:hrows * (h + 1), :], pad_ref[:, LANES * h:LANES * (h + 1)])
             for h in range(H_B)], axis=0)
        s = jnp.where(_new_row_mask(n_new, inclusive=True), s + bias_ref[0], NEG_INF)
        m = jnp.max(s, axis=-1, keepdims=True)
        p = jnp.exp(s - m)
        pad_ref[0:n_new, :] = vn_ref[0]
        acc_ref[...] = jnp.concatenate(
            [jnp.dot(p[hrows * h:hrows * (h + 1)], pad_ref[:, LANES * h:LANES * (h + 1)],
                     preferred_element_type=F32) for h in range(H_B)], axis=0)
        m_ref[...] = m
        l_ref[...] = jnp.sum(p, axis=-1, keepdims=True)

    far = bias_ref[2]
    far2 = jnp.concatenate([far, far], axis=1)
    near2 = jnp.concatenate([far, jnp.where(j == 0, bias_ref[1], far)], axis=1)
    blocks = []
    for a in range(PAGE_PAIRS):
        rows = []
        for h in range(H_B):
            kh = jnp.concatenate([head_rows(k_refs[2 * a + 1], h), head_rows(k_refs[2 * a], h)], axis=0)
            rows.append(_nt(qbd_ref[hrows * h:hrows * (h + 1), :], kh))
        blocks.append(jnp.concatenate(rows, axis=0) + (near2 if a == 0 else far2))
    m_old = m_ref[...]
    m_new = m_old
    for a in range(PAGE_PAIRS):
        m_new = jnp.maximum(m_new, jnp.max(blocks[a], axis=-1, keepdims=True))
    alpha = jnp.exp(m_old - m_new)
    l_new = alpha * l_ref[...]
    pv = [None] * H_B
    for a in range(PAGE_PAIRS):
        p = jnp.exp(blocks[a] - m_new)
        l_new = l_new + jnp.sum(p, axis=-1, keepdims=True)
        for h in range(H_B):
            vh = jnp.concatenate([head_rows(v_refs[2 * a + 1], h), head_rows(v_refs[2 * a], h)], axis=0)
            d = jnp.dot(p[hrows * h:hrows * (h + 1)], vh, preferred_element_type=F32)
            pv[h] = d if pv[h] is None else pv[h] + d
    acc_ref[...] = alpha * acc_ref[...] + jnp.concatenate(pv, axis=0)
    l_ref[...] = l_new
    m_ref[...] = m_new

    @pl.when(j == pl.num_programs(1) - 1)
    def _():
        o = acc_ref[...] / l_ref[...]
        g = g_ref[...]
        parts = []
        for h in range(H_B):
            oh = o[hrows * h:hrows * h + n_new] - sc_ref[0] * o[hrows * h + n_new:hrows * (h + 1)]
            parts.append(_subln(oh, g, sc_ref[1]))
        o_ref[0] = jnp.concatenate(parts, axis=-1) * _silu(z_ref[0])


def _diff_decode(pt_flat, scalars, n_pages, layer, q3, kn3, vn3, zb3, bias_dec, subln_g,
                 cache_k, cache_v):
    nb, n_new, _ = kn3.shape
    per_req = lambda w, cb: pl.BlockSpec((1, n_new, w), lambda b, j, pt, sc: (b, 0, cb))
    grid_spec = pltpu.PrefetchScalarGridSpec(
        num_scalar_prefetch=2,
        grid=(nb, n_pages // PAGES_PER_STEP),
        in_specs=[per_req(512, QB0 // N_GROUPS), per_req(512, 0), per_req(512, 0), per_req(512, 0),
                  pl.BlockSpec((3, DEC_ROWS, PAGE_SIZE), lambda b, j, pt, sc: (0, 0, 0)),
                  pl.BlockSpec((1, LANES), lambda b, j, pt, sc: (0, 0))]
                 + [_page_spec(layer, n_pages, p) for p in range(PAGES_PER_STEP)] * 2,
        out_specs=per_req(512, 0),
        scratch_shapes=[pltpu.VMEM((DEC_ROWS, LANES), F32), pltpu.VMEM((DEC_ROWS, LANES), F32),
                        pltpu.VMEM((DEC_ROWS, 1), F32), pltpu.VMEM((DEC_ROWS, 1), F32),
                        pltpu.VMEM((PAGE_SIZE, 512), F32)],
    )
    return pl.pallas_call(
        _diff_decode_kernel,
        grid_spec=grid_spec,
        out_shape=jax.ShapeDtypeStruct((nb, n_new, W_B), F32),
        compiler_params=pltpu.CompilerParams(
            dimension_semantics=("arbitrary", "arbitrary"), vmem_limit_bytes=VMEM_LIMIT),
        name="diff_decode",
    )(pt_flat, scalars, q3, kn3, vn3, zb3, bias_dec, subln_g,
      *([cache_k] * PAGES_PER_STEP), *([cache_v] * PAGES_PER_STEP))


def _merge_kernel(x_ref, oa_ref, ob_ref, ga_ref, gb_ref, wa_ref, wb_ref, wo_ref, fg_ref, y_ref,
                  *, final):
    a = jnp.dot(oa_ref[...].astype(BF16), wa_ref[...], preferred_element_type=F32)
    b = jnp.dot(ob_ref[...].astype(BF16), wb_ref[...], preferred_element_type=F32)
    merged = _sigmoid(ga_ref[...]) * a + _sigmoid(gb_ref[...]) * b
    y = x_ref[...] + jnp.dot(merged.astype(BF16), wo_ref[...], preferred_element_type=F32)
    if final:
        ms = jnp.mean(y * y, axis=-1, keepdims=True)
        y = (y * lax.rsqrt(ms + RMS_EPS)) * fg_ref[...]
    y_ref[...] = y


def _merge(x, oa, ob, ga, gb, wa, wb, wo, fg, final, tm):
    t = x.shape[0]
    row = lambda w: pl.BlockSpec((tm, w), lambda i: (i, 0))
    whole = lambda a: pl.BlockSpec(a.shape, lambda i: (0, 0))
    return pl.pallas_call(
        functools.partial(_merge_kernel, final=final),
        grid=(t // tm,),
        in_specs=[row(D_MODEL), row(W_A), row(W_B), row(D_MODEL), row(D_MODEL),
                  whole(wa), whole(wb), whole(wo), whole(fg)],
        out_specs=row(D_MODEL),
        out_shape=jax.ShapeDtypeStruct((t, D_MODEL), F32),
        compiler_params=pltpu.CompilerParams(
            dimension_semantics=("arbitrary",), vmem_limit_bytes=VMEM_LIMIT),
        name="merge",
    )(x, oa, ob, ga, gb, wa, wb, wo, fg)


def _bias_by_distance(rel_bias, n):
    d = jnp.arange(n)
    nf = jnp.maximum(d, 1).astype(F32)
    large = MAX_EXACT + (jnp.log(nf / MAX_EXACT) / math.log(MAX_DISTANCE / MAX_EXACT)
                         * (N_BUCKETS - MAX_EXACT)).astype(jnp.int32)
    large = jnp.minimum(large, N_BUCKETS - 1)
    return rel_bias[jnp.where(d < MAX_EXACT, d, large)].astype(F32)


def _strict_tri(n, upper):
    a = lax.broadcasted_iota(jnp.int32, (n, n), 0)
    b = lax.broadcasted_iota(jnp.int32, (n, n), 1)
    return ((b > a) if upper else (a > b)).astype(BF16)


def kernel(x_prompt, x_sample, cache_sb_k, cache_sb_v, cache_diff_k, cache_diff_v, page_table,
           norm_g, w_in, lambda_q1, lambda_k1, lambda_q2, lambda_k2, subln_g, w_up_a, w_up_b,
           w_out, rel_bias, final_norm_g):
    depth = w_in.shape[0]
    nb_p, t_p, _ = x_prompt.shape
    nb_s, n_new, _ = x_sample.shape
    n_pages = page_table.shape[1]
    n_pool = cache_sb_k.shape[1]
    assert nb_p == 1 and t_p % Q_BLK == 0 and Q_BLK == 2 * K_BLK and K_BLK >= MAX_DISTANCE
    assert t_p >= FAR_GROUP * K_BLK and FAR_GROUP % 2 == 0
    assert n_new * H_A == DEC_ROWS and n_new * 2 * H_B == DEC_ROWS
    assert n_pages % PAGES_PER_STEP == 0 and PAGE_SIZE >= MAX_DISTANCE

    xp = x_prompt.reshape(t_p, D_MODEL)
    xs = x_sample.reshape(nb_s * n_new, D_MODEL)
    pt_flat = page_table.reshape(-1).astype(jnp.int32)
    caches = [jnp.transpose(c, (0, 1, 3, 4, 2)).reshape(depth, n_pool, 512, PAGE_SIZE)
              for c in (cache_sb_k, cache_sb_v)]
    caches += [c.reshape(depth, n_pool, 512, PAGE_SIZE) for c in (cache_diff_k, cache_diff_v)]
    ks = lax.broadcasted_iota(jnp.int32, (K_BLK, K_BLK), 0)
    kj = lax.broadcasted_iota(jnp.int32, (K_BLK, K_BLK), 1)
    tri_p = -(kj >= ks).astype(BF16)
    tri_1 = _strict_tri(PAGE_SIZE, upper=False)
    tri_2 = _strict_tri(2 * PAGE_SIZE, upper=False)
    fg = final_norm_g.reshape(1, D_MODEL).astype(F32)

    n_tab = Q_BLK + K_BLK
    tab = _bias_by_distance(rel_bias, n_tab + 1)
    far = tab[-1]
    ext = jnp.transpose(tab[jnp.clip(jnp.arange(n_tab + Q_BLK) - (Q_BLK - 1), 0, n_tab)])
    ext_len = n_tab + Q_BLK
    circ = jnp.tile(ext, (1, K_BLK + 1))[:, :K_BLK * (ext_len - 1)].reshape(H_B, K_BLK, ext_len - 1)
    bias_near = jnp.stack([circ[:, :, off + Q_BLK - 1:off + 2 * Q_BLK - 1] for off in (-K_BLK, 0, K_BLK)],
                          axis=1)
    rr = jnp.arange(DEC_ROWS)
    qi = rr % n_new
    hh = rr // (2 * n_new)
    ss = jnp.arange(PAGE_SIZE)
    bias_new = tab[jnp.maximum(qi[:, None] - ss[None, :], 0), hh[:, None]]
    bias_last = tab[PAGE_SIZE + qi[:, None] - ss[None, :], hh[:, None]]
    bias_far = jnp.broadcast_to(far[hh][:, None], (DEC_ROWS, PAGE_SIZE))
    bias_dec = jnp.stack([bias_new, bias_last, bias_far]).astype(F32)

    rows_p = ([], [], [], [])
    rows_s = ([], [], [], [])
    for i in range(depth):
        lam_init = 0.8 - 0.6 * math.exp(-0.3 * i)
        lam = (jnp.exp(jnp.sum(lambda_q1[i].astype(F32) * lambda_k1[i].astype(F32)))
               - jnp.exp(jnp.sum(lambda_q2[i].astype(F32) * lambda_k2[i].astype(F32)))
               + lam_init)
        scalars = jnp.concatenate([jnp.stack([lam, jnp.asarray(1.0 - lam_init, F32)]), far]).astype(F32)
        g_in = norm_g[i].reshape(1, D_MODEL).astype(F32)
        w_bf = w_in[i].astype(BF16)
        wa, wb, wo = w_up_a[i].astype(BF16), w_up_b[i].astype(BF16), w_out[i].astype(BF16)
        sg = subln_g[i].reshape(1, LANES).astype(F32)
        final = i == depth - 1

        qk, vat, vbt, ka, va, kb, vb, za, zb, ga, gb = _inproj(xp, g_in, w_bf, True, K_BLK)
        oa = _sb_prompt(qk, vat, za, tri_p)
        ob = _diff_prompt(scalars, qk, vbt, zb, bias_near, sg)
        xp = _merge(xp, oa, ob, ga, gb, wa, wb, wo, fg, final, 512)
        for lst, a, hn in zip(rows_p, (ka, va, kb, vb), (H_A, H_A, H_B, H_B)):
            lst.append(a.reshape(nb_p, t_p, hn, 512 // hn))

        qk, ka, va, kb, vb, za, zb, ga, gb = _inproj(xs, g_in, w_bf, False, nb_s * n_new)
        r3 = lambda a: a.reshape(nb_s, n_new, a.shape[-1])
        oa = _sb_decode(pt_flat, n_pages, i, r3(qk), r3(ka), r3(va), r3(za), tri_1, tri_2,
                        caches[0], caches[1])
        ob = _diff_decode(pt_flat, scalars, n_pages, i, r3(qk), r3(kb), r3(vb), r3(zb), bias_dec, sg,
                          caches[2], caches[3])
        xs = _merge(xs, oa.reshape(-1, W_A), ob.reshape(-1, W_B), ga, gb, wa, wb, wo, fg, final,
                    nb_s * n_new)
        for lst, a, hn in zip(rows_s, (ka, va, kb, vb), (H_A, H_A, H_B, H_B)):
            lst.append(a.reshape(nb_s, n_new, hn, 512 // hn))

    y_prompt = xp.reshape(nb_p, t_p, D_MODEL)
    y_sample = xs.reshape(nb_s, n_new, D_MODEL)
    return (y_prompt, y_sample,
            jnp.stack(rows_p[0]), jnp.stack(rows_p[1]), jnp.stack(rows_p[2]), jnp.stack(rows_p[3]),
            jnp.stack(rows_s[0]), jnp.stack(rows_s[1]), jnp.stack(rows_s[2]), jnp.stack(rows_s[3]))
```

```python
import functools
import math

import jax
import jax.numpy as jnp
from jax import lax
from jax.experimental import pallas as pl
from jax.experimental.pallas import tpu as pltpu

F32 = jnp.float32
BF16 = jnp.bfloat16

D_MODEL = 1024
H_A, HD_A = 8, 64
H_B, HD_B = 4, 64
W_A = H_A * HD_A
W_B = H_B * 2 * HD_B
N_BUCKETS, MAX_EXACT, MAX_DISTANCE = 32, 16, 128
RMS_EPS, SUBLN_EPS = 1e-6, 1e-5
NEG_INF = -1e30
PAGE_SIZE = 128
QK_SCALE = 0.125

LANES = 128
HALF = 64
VMEM_LIMIT = 48 * 1024 * 1024

QA0, KA0, QB0, KB0 = 0, 4, 8, 12
N_GROUPS = W_A // LANES

Q_BLK = 512
K_BLK = 256
FAR_GROUP = 4
ONES_ROWS = 16
PAGES_PER_STEP = 32
PAGE_PAIRS = PAGES_PER_STEP // 2
DEC_ROWS = 64


def _silu(z):
    return z / (1.0 + jnp.exp(-z))


def _sigmoid(z):
    return 1.0 / (1.0 + jnp.exp(-z))


def _softplus(z):
    neg_abs = lax.bitcast_convert_type(
        lax.bitcast_convert_type(z, jnp.uint32) | jnp.uint32(0x80000000), F32)
    return jnp.maximum(z, 0.0) + jnp.log(1.0 + jnp.exp(neg_abs))


def _softplus_bf16(z):
    zb = z.astype(BF16)
    neg_abs = lax.bitcast_convert_type(
        lax.bitcast_convert_type(zb, jnp.uint16) | jnp.uint16(0x8000), BF16)
    one = jnp.asarray(1.0, BF16)
    return jnp.maximum(zb, jnp.zeros_like(zb)) + jnp.log(one + jnp.exp(neg_abs))


def _log_sigmoids(z):
    sp = jnp.log(1.0 + jnp.exp(-jnp.abs(z)))
    lb = jnp.minimum(z, 0.0) - sp
    return lb, lb - z


def _split_bf16(x):
    hi = x.astype(BF16)
    return hi, (x - hi.astype(F32)).astype(BF16)


def _nt(a, b):
    return lax.dot_general(a, b, (((1,), (1,)), ((), ())), preferred_element_type=F32)


def _subln(o, g, scale):
    ms = jnp.mean(o * o, axis=-1, keepdims=True)
    return ((o * lax.rsqrt(ms + SUBLN_EPS)) * g) * scale


def _inproj_kernel(x_ref, g_ref, w_ref, *outs, prompt):
    if prompt:
        qk_ref, vat_ref, vbt_ref, ka_ref, va_ref, kb_ref, vb_ref, za_ref, zb_ref, ga_ref, gb_ref = outs
    else:
        qk_ref, ka_ref, va_ref, kb_ref, vb_ref, za_ref, zb_ref, ga_ref, gb_ref = outs
    x = x_ref[...]
    ms = jnp.mean(x * x, axis=-1, keepdims=True)
    h = ((x * lax.rsqrt(ms + RMS_EPS)) * g_ref[...]).astype(BF16)

    def mm(c0, width):
        return jnp.dot(h, w_ref[:, c0:c0 + width], preferred_element_type=F32)

    qdt = qk_ref.dtype
    qk_ref[:, 0:512] = (mm(0, 512) * QK_SCALE).astype(qdt)
    ka = mm(512, 512)
    ka_ref[...] = ka
    qk_ref[:, 512:1024] = ka.astype(qdt)
    va = mm(1024, 512)
    va_ref[...] = va
    za_ref[...] = mm(1536, 512)
    qk_ref[:, 1024:1536] = (mm(2048, 512) * QK_SCALE).astype(qdt)
    kb = mm(2560, 512)
    kb_ref[...] = kb
    qk_ref[:, 1536:2048] = kb.astype(qdt)
    vb = mm(3072, 512)
    vb_ref[...] = vb
    zb_ref[...] = mm(3584, 512)
    ga_ref[...] = mm(4096, 1024)
    gb_ref[...] = mm(5120, 1024)
    if prompt:
        vat_ref[0] = va.T.astype(BF16)
        vbt_ref[0] = vb.T.astype(BF16)


def _inproj(x, g, w_bf16, prompt, tm):
    t = x.shape[0]
    d_in = w_bf16.shape[1]
    row = lambda w: pl.BlockSpec((tm, w), lambda i: (i, 0))
    out_shape = [jax.ShapeDtypeStruct((t, 2048), BF16 if prompt else F32)]
    out_specs = [row(2048)]
    if prompt:
        out_shape += [jax.ShapeDtypeStruct((t // tm, 512, tm), BF16)] * 2
        out_specs += [pl.BlockSpec((1, 512, tm), lambda i: (i, 0, 0))] * 2
    out_shape += [jax.ShapeDtypeStruct((t, 512), F32)] * 6 + [jax.ShapeDtypeStruct((t, 1024), F32)] * 2
    out_specs += [row(512)] * 6 + [row(1024)] * 2
    return pl.pallas_call(
        functools.partial(_inproj_kernel, prompt=prompt),
        grid=(t // tm,),
        in_specs=[row(D_MODEL),
                  pl.BlockSpec((1, D_MODEL), lambda i: (0, 0)),
                  pl.BlockSpec((D_MODEL, d_in), lambda i: (0, 0))],
        out_specs=out_specs,
        out_shape=out_shape,
        compiler_params=pltpu.CompilerParams(
            dimension_semantics=("arbitrary",), vmem_limit_bytes=VMEM_LIMIT),
        name="inproj",
    )(x, g, w_bf16)


def _load_queries_t(q_ref, qt_ref):
    qt = q_ref[...].astype(F32).T
    d = lax.broadcasted_iota(jnp.int32, qt.shape, 0)
    qt_ref[:, 0:Q_BLK] = jnp.where(d < HALF, qt, 0.0).astype(BF16)
    qt_ref[:, Q_BLK:2 * Q_BLK] = jnp.where(d >= HALF, qt, 0.0).astype(BF16)


def _key_query_positions(kb, i):
    kpos = kb * K_BLK + lax.broadcasted_iota(jnp.int32, (K_BLK, 2 * Q_BLK), 0)
    qpos = i * Q_BLK + (lax.broadcasted_iota(jnp.int32, (K_BLK, 2 * Q_BLK), 1) & (Q_BLK - 1))
    return kpos, qpos


def _sb_prompt_kernel(q_ref, k_ref, vt_ref, z_ref, tri_ref, o_ref, qt_ref, acc_ref, car_ref,
                      sa_ref, sb_ref):
    i = pl.program_id(1)
    _load_queries_t(q_ref, qt_ref)
    acc_ref[...] = jnp.zeros_like(acc_ref)
    car_ref[...] = jnp.zeros_like(car_ref)

    def raw_scores(kb_right):
        return tuple(
            jnp.dot(k_ref[pl.ds(pl.multiple_of((kb_right - b) * K_BLK, K_BLK), K_BLK), :], qt_ref[...],
                    preferred_element_type=F32) for b in range(2))

    def pair(state, kb_right, masked, raw=None):
        n = 2
        if raw is None:
            raw = raw_scores(kb_right)
        car, acc = state
        ws = []
        for kb, z in zip([kb_right - b for b in range(n)], raw):
            if masked:
                kpos, qpos = _key_query_positions(kb, i)
                mask = kpos < qpos
                nlk = jnp.where(mask, _softplus(z), 0.0).astype(BF16)
            else:
                nlk = _softplus_bf16(z)
            incl = jnp.dot(tri_ref[...], nlk, preferred_element_type=F32)
            w = jnp.exp(z + incl + car)
            if masked:
                w = jnp.where(mask, w, 0.0)
            ws.append(w.astype(BF16))
            car = car + incl[0:1, :]
        vt = jnp.concatenate([vt_ref[kb_right - b] for b in range(n)], axis=1)
        return car, acc + jnp.dot(vt, jnp.concatenate(ws, axis=0), preferred_element_type=F32)

    def load_state():
        return car_ref[...], acc_ref[...]

    def store_state(state):
        car_ref[...], acc_ref[...] = state

    store_state(pair(load_state(), 2 * i + 1, True))

    def score_ahead(buf_ref, p):
        left = jnp.maximum(2 * i - 1 - 2 * p, 1) - 1
        k2 = k_ref[pl.ds(pl.multiple_of(left * K_BLK, K_BLK), 2 * K_BLK), :]
        buf_ref[...] = jnp.dot(k2, qt_ref[...], preferred_element_type=F32)

    def update_from(state, buf_ref, p):
        raw = (buf_ref[K_BLK:2 * K_BLK, :], buf_ref[0:K_BLK, :])
        return pair(state, 2 * i - 1 - 2 * p, False, raw)

    score_ahead(sa_ref, 0)

    def body(j, carry):
        score_ahead(sb_ref, 2 * j + 1)
        state = update_from(load_state(), sa_ref, 2 * j)
        score_ahead(sa_ref, 2 * j + 2)
        store_state(update_from(state, sb_ref, 2 * j + 1))
        return carry

    lax.fori_loop(0, i // 2, body, 0)

    @pl.when(i % 2 == 1)
    def _():
        store_state(update_from(load_state(), sa_ref, i - 1))

    row = lax.broadcasted_iota(jnp.int32, (LANES, Q_BLK), 0)
    ot = jnp.where(row < HALF, acc_ref[:, 0:Q_BLK], acc_ref[:, Q_BLK:2 * Q_BLK])
    o_ref[...] = (ot.T * _silu(z_ref[...])).astype(o_ref.dtype)


def _sb_prompt(qk, vat, za, tri):
    t = qk.shape[0]
    nkb = vat.shape[0]
    return pl.pallas_call(
        _sb_prompt_kernel,
        grid=(N_GROUPS, t // Q_BLK),
        in_specs=[pl.BlockSpec((Q_BLK, LANES), lambda g, i: (i, QA0 + g)),
                  pl.BlockSpec((t, LANES), lambda g, i: (0, KA0 + g)),
                  pl.BlockSpec((nkb, LANES, K_BLK), lambda g, i: (0, g, 0)),
                  pl.BlockSpec((Q_BLK, LANES), lambda g, i: (i, g)),
                  pl.BlockSpec((K_BLK, K_BLK), lambda g, i: (0, 0))],
        out_specs=pl.BlockSpec((Q_BLK, LANES), lambda g, i: (i, g)),
        out_shape=jax.ShapeDtypeStruct((t, W_A), BF16),
        scratch_shapes=[pltpu.VMEM((LANES, 2 * Q_BLK), BF16), pltpu.VMEM((LANES, 2 * Q_BLK), F32),
                        pltpu.VMEM((1, 2 * Q_BLK), F32),
                        pltpu.VMEM((2 * K_BLK, 2 * Q_BLK), F32), pltpu.VMEM((2 * K_BLK, 2 * Q_BLK), F32)],
        compiler_params=pltpu.CompilerParams(
            dimension_semantics=("arbitrary", "arbitrary"), vmem_limit_bytes=VMEM_LIMIT),
        name="sb_prompt",
    )(qk, qk, vat, za, tri)


def _diff_prompt_kernel(sc_ref, q_ref, k_ref, vt_ref, z_ref, bias_ref, g_ref, o_ref,
                        qt_ref, acc_ref, m_ref, sa_ref, sb_ref):
    h = pl.program_id(0)
    i = pl.program_id(1)
    _load_queries_t(q_ref, qt_ref)
    acc_ref[...] = jnp.zeros_like(acc_ref)
    m_ref[...] = jnp.full_like(m_ref, NEG_INF)
    ones = jnp.ones((ONES_ROWS, K_BLK), BF16)
    far = sc_ref[2 + h]

    def raw_scores(kb_right, n):
        return tuple(
            jnp.dot(k_ref[pl.ds(pl.multiple_of((kb_right - b) * K_BLK, K_BLK), K_BLK), :], qt_ref[...],
                    preferred_element_type=F32) for b in range(n))

    def group(state, kb_right, nears, masked, raw=None):
        m_old, acc = state
        n = len(nears)
        all_far = all(near is None for near in nears)
        if raw is None:
            raw = raw_scores(kb_right, n)
        ss = []
        for kb, near, s in zip([kb_right - b for b in range(n)], nears, raw):
            if near is not None:
                b = bias_ref[0, near]
                s = s + jnp.concatenate([b, b], axis=1)
            elif not all_far:
                s = s + far
            if masked:
                kpos, qpos = _key_query_positions(kb, i)
                s = jnp.where(kpos <= qpos, s, NEG_INF)
            ss.append(s)
        top = jnp.max(ss[0], axis=0, keepdims=True)
        for s in ss[1:]:
            top = jnp.maximum(top, jnp.max(s, axis=0, keepdims=True))
        if all_far:
            m_new = jnp.maximum(m_old, top + far)
            shift = m_new - far
        else:
            m_new = jnp.maximum(m_old, top)
            shift = m_new
        alpha = jnp.exp(m_old - m_new)
        ps = [jnp.exp(s - shift) for s in ss]
        vt = jnp.concatenate(
            [jnp.concatenate([vt_ref[kb_right - b], ones], axis=0) for b in range(n)], axis=1)
        pcat = jnp.concatenate([p.astype(BF16) for p in ps], axis=0)
        return m_new, alpha * acc + jnp.dot(vt, pcat, preferred_element_type=F32)

    def load_state():
        return m_ref[...], acc_ref[...]

    def store_state(state):
        m_ref[...], acc_ref[...] = state

    store_state(group(load_state(), 2 * i + 1, (0, 1), True))

    @pl.when(i > 0)
    def _():
        store_state(group(load_state(), 2 * i - 1, (2, None), False))

    n_pairs = jnp.maximum(i - 1, 0)

    def score_ahead(buf_ref, p):
        left = jnp.maximum(2 * i - 3 - 2 * p, 1) - 1
        k2 = k_ref[pl.ds(pl.multiple_of(left * K_BLK, K_BLK), 2 * K_BLK), :]
        buf_ref[...] = jnp.dot(k2, qt_ref[...], preferred_element_type=F32)

    def update_from(state, buf_ref, p):
        raw = (buf_ref[K_BLK:2 * K_BLK, :], buf_ref[0:K_BLK, :])
        return group(state, 2 * i - 3 - 2 * p, (None, None), False, raw)

    score_ahead(sa_ref, 0)

    def body(j, carry):
        score_ahead(sb_ref, 2 * j + 1)
        state = update_from(load_state(), sa_ref, 2 * j)
        score_ahead(sa_ref, 2 * j + 2)
        store_state(update_from(state, sb_ref, 2 * j + 1))
        return carry

    lax.fori_loop(0, n_pairs // 2, body, 0)

    @pl.when(n_pairs % 2 == 1)
    def _():
        store_state(update_from(load_state(), sa_ref, n_pairs - 1))

    inv = 1.0 / acc_ref[LANES:LANES + 1, :]
    ot = (acc_ref[0:LANES, 0:Q_BLK] * inv[:, 0:Q_BLK]
          - sc_ref[0] * (acc_ref[0:LANES, Q_BLK:2 * Q_BLK] * inv[:, Q_BLK:2 * Q_BLK]))
    o = _subln(ot.T, g_ref[...], sc_ref[1])
    o_ref[...] = (o * _silu(z_ref[...])).astype(o_ref.dtype)


def _diff_prompt(scalars, qk, vbt, zb, bias_near, subln_g):
    t = qk.shape[0]
    nkb = vbt.shape[0]
    grid_spec = pltpu.PrefetchScalarGridSpec(
        num_scalar_prefetch=1,
        grid=(H_B, t // Q_BLK),
        in_specs=[pl.BlockSpec((Q_BLK, LANES), lambda h, i, sc: (i, QB0 + h)),
                  pl.BlockSpec((t, LANES), lambda h, i, sc: (0, KB0 + h)),
                  pl.BlockSpec((nkb, LANES, K_BLK), lambda h, i, sc: (0, h, 0)),
                  pl.BlockSpec((Q_BLK, LANES), lambda h, i, sc: (i, h)),
                  pl.BlockSpec((1, 3, K_BLK, Q_BLK), lambda h, i, sc: (h, 0, 0, 0)),
                  pl.BlockSpec((1, LANES), lambda h, i, sc: (0, 0))],
        out_specs=pl.BlockSpec((Q_BLK, LANES), lambda h, i, sc: (i, h)),
        scratch_shapes=[pltpu.VMEM((LANES, 2 * Q_BLK), BF16),
                        pltpu.VMEM((LANES + ONES_ROWS, 2 * Q_BLK), F32),
                        pltpu.VMEM((1, 2 * Q_BLK), F32),
                        pltpu.VMEM((2 * K_BLK, 2 * Q_BLK), F32), pltpu.VMEM((2 * K_BLK, 2 * Q_BLK), F32)],
    )
    return pl.pallas_call(
        _diff_prompt_kernel,
        grid_spec=grid_spec,
        out_shape=jax.ShapeDtypeStruct((t, W_B), BF16),
        compiler_params=pltpu.CompilerParams(
            dimension_semantics=("arbitrary", "arbitrary"), vmem_limit_bytes=VMEM_LIMIT),
        name="diff_prompt",
    )(scalars, qk, qk, vbt, zb, bias_near, subln_g)


def _block_diag_queries(q):
    n = q.shape[0]
    qt = jnp.concatenate([q] * (DEC_ROWS // n), axis=0)
    r = lax.broadcasted_iota(jnp.int32, qt.shape, 0)
    l = lax.broadcasted_iota(jnp.int32, qt.shape, 1)
    return jnp.where((l // HALF) == (r // n), qt, 0.0)


def _fold_rows(x, lanes_per_row_group, rows_per_group):
    r = lax.broadcasted_iota(jnp.int32, x.shape, 0)
    l = lax.broadcasted_iota(jnp.int32, x.shape, 1)
    x = jnp.where((l // lanes_per_row_group) == (r // rows_per_group), x, 0.0)
    out = x[0:8]
    for a in range(1, x.shape[0] // 8):
        out = out + x[8 * a:8 * a + 8]
    return out


def _new_row_mask(n_new, inclusive):
    r = lax.broadcasted_iota(jnp.int32, (DEC_ROWS, PAGE_SIZE), 0)
    s = lax.broadcasted_iota(jnp.int32, (DEC_ROWS, PAGE_SIZE), 1)
    i = r % n_new
    return (s <= i) if inclusive else (s < i)


def _sb_decode_kernel(pt_ref, q_ref, kn_ref, vn_ref, z_ref, tri1_ref, tri2_ref, *rest):
    npg = PAGES_PER_STEP
    kt_refs, vt_refs = rest[:npg], rest[npg:2 * npg]
    o_ref, qbd_ref, acc_ref, car_ref, pad_ref = rest[2 * npg:]
    j = pl.program_id(1)
    n_new = kn_ref.shape[1]

    @pl.when(j == 0)
    def _():
        qbd_ref[...] = _block_diag_queries(q_ref[0])
        mask = _new_row_mask(n_new, inclusive=False)
        pad_ref[...] = jnp.zeros_like(pad_ref)
        pad_ref[0:n_new, :] = kn_ref[0]
        lb, lk = _log_sigmoids(_nt(qbd_ref[...], pad_ref[...]))
        lk = jnp.where(mask, lk, 0.0)
        hi, lo = _split_bf16(lk)
        tri = tri1_ref[...]
        later = jnp.dot(hi, tri, preferred_element_type=F32) + jnp.dot(lo, tri, preferred_element_type=F32)
        w = jnp.where(mask, jnp.exp(lb + later), 0.0)
        pad_ref[0:n_new, :] = vn_ref[0]
        acc_ref[...] = jnp.dot(w, pad_ref[...], preferred_element_type=F32)
        car_ref[...] = jnp.sum(lk, axis=-1, keepdims=True)

    qbd = qbd_ref[...]
    z = jnp.concatenate(
        [jnp.dot(qbd, jnp.concatenate([kt_refs[2 * a + 1][...], kt_refs[2 * a][...]], axis=1),
                 preferred_element_type=F32) for a in range(PAGE_PAIRS)], axis=0)
    lb, lk = _log_sigmoids(z)
    hi, lo = _split_bf16(lk)
    tri = tri2_ref[...]
    later = jnp.dot(hi, tri, preferred_element_type=F32) + jnp.dot(lo, tri, preferred_element_type=F32)
    rs = jnp.sum(lk, axis=-1, keepdims=True)
    car = car_ref[...]
    cars = []
    for a in range(PAGE_PAIRS):
        cars.append(car)
        car = car + rs[DEC_ROWS * a:DEC_ROWS * (a + 1)]
    car_ref[...] = car
    w = jnp.exp(lb + later + jnp.concatenate(cars, axis=0))
    contrib = acc_ref[...]
    for a in range(PAGE_PAIRS):
        vt = jnp.concatenate([vt_refs[2 * a + 1][...], vt_refs[2 * a][...]], axis=1)
        contrib = contrib + _nt(w[DEC_ROWS * a:DEC_ROWS * (a + 1)], vt)
    acc_ref[...] = contrib

    @pl.when(j == pl.num_programs(1) - 1)
    def _():
        o = _fold_rows(acc_ref[...], HALF, n_new)
        o_ref[0] = o * _silu(z_ref[0])


def _page_spec(layer, n_pages, p):
    def index_map(b, j, pt, *_):
        page = n_pages - 1 - (j * PAGES_PER_STEP + p)
        return (layer, pt[b * n_pages + page], 0, 0)
    return pl.BlockSpec((None, None, 512, PAGE_SIZE), index_map)


def _sb_decode(pt_flat, n_pages, layer, q3, kn3, vn3, za3, tri1, tri2, cache_k, cache_v):
    nb, n_new, _ = kn3.shape
    per_req = lambda w, cb: pl.BlockSpec((1, n_new, w), lambda b, j, pt: (b, 0, cb))
    whole = lambda a: pl.BlockSpec(a.shape, lambda b, j, pt: (0, 0))
    grid_spec = pltpu.PrefetchScalarGridSpec(
        num_scalar_prefetch=1,
        grid=(nb, n_pages // PAGES_PER_STEP),
        in_specs=[per_req(512, QA0 // N_GROUPS), per_req(512, 0), per_req(512, 0), per_req(512, 0),
                  whole(tri1), whole(tri2)]
                 + [_page_spec(layer, n_pages, p) for p in range(PAGES_PER_STEP)] * 2,
        out_specs=per_req(512, 0),
        scratch_shapes=[pltpu.VMEM((DEC_ROWS, 512), F32), pltpu.VMEM((DEC_ROWS, 512), F32),
                        pltpu.VMEM((DEC_ROWS, 1), F32), pltpu.VMEM((PAGE_SIZE, 512), F32)],
    )
    return pl.pallas_call(
        _sb_decode_kernel,
        grid_spec=grid_spec,
        out_shape=jax.ShapeDtypeStruct((nb, n_new, W_A), F32),
        compiler_params=pltpu.CompilerParams(
            dimension_semantics=("arbitrary", "arbitrary"), vmem_limit_bytes=VMEM_LIMIT),
        name="sb_decode",
    )(pt_flat, q3, kn3, vn3, za3, tri1, tri2,
      *([cache_k] * PAGES_PER_STEP), *([cache_v] * PAGES_PER_STEP))


def _diff_decode_kernel(pt_ref, sc_ref, q_ref, kn_ref, vn_ref, z_ref, bias_ref, g_ref, *rest):
    npg = PAGES_PER_STEP
    k_refs, v_refs = rest[:npg], rest[npg:2 * npg]
    o_ref, qbd_ref, acc_ref, m_ref, l_ref, pad_ref = rest[2 * npg:]
    j = pl.program_id(1)
    n_new = kn_ref.shape[1]
    hrows = 2 * n_new

    def head_rows(ref, h):
        return ref[pl.ds(h, PAGE_SIZE, stride=H_B), :]

    @pl.when(j == 0)
    def _():
        q = q_ref[0]
        lane = lax.broadcasted_iota(jnp.int32, (n_new, LANES), 1)
        for h in range(H_B):
            qh = q[:, LANES * h:LANES * (h + 1)]
            for c in range(2):
                qbd_ref[hrows * h + n_new * c:hrows * h + n_new * (c + 1), :] = jnp.where(
                    (lane // HALF) == c, qh, 0.0)
        pad_ref[...] = jnp.zeros_like(pad_ref)
        pad_ref[0:n_new, :] = kn_ref[0]
        s = jnp.concatenate(
            [_nt(qbd_ref[hrows * h:hrows * (h + 1), :], pad_ref[:, LANES * h:LANES * (h + 1)])
             for h in range(H_B)], axis=0)
        s = jnp.where(_new_row_mask(n_new, inclusive=True), s + bias_ref[0], NEG_INF)
        m = jnp.max(s, axis=-1, keepdims=True)
        p = jnp.exp(s - m)
        pad_ref[0:n_new, :] = vn_ref[0]
        acc_ref[...] = jnp.concatenate(
            [jnp.dot(p[hrows * h:hrows * (h + 1)], pad_ref[:, LANES * h:LANES * (h + 1)],
                     preferred_element_type=F32) for h in range(H_B)], axis=0)
        m_ref[...] = m
        l_ref[...] = jnp.sum(p, axis=-1, keepdims=True)

    far = bias_ref[2]
    far2 = jnp.concatenate([far, far], axis=1)
    near2 = jnp.concatenate([far, jnp.where(j == 0, bias_ref[1], far)], axis=1)
    blocks = []
    for a in range(PAGE_PAIRS):
        rows = []
        for h in range(H_B):
            kh = jnp.concatenate([head_rows(k_refs[2 * a + 1], h), head_rows(k_refs[2 * a], h)], axis=0)
            rows.append(_nt(qbd_ref[hrows * h:hrows * (h + 1), :], kh))
        blocks.append(jnp.concatenate(rows, axis=0) + (near2 if a == 0 else far2))
    m_old = m_ref[...]
    m_new = m_old
    for a in range(PAGE_PAIRS):
        m_new = jnp.maximum(m_new, jnp.max(blocks[a], axis=-1, keepdims=True))
    alpha = jnp.exp(m_old - m_new)
    l_new = alpha * l_ref[...]
    pv = [None] * H_B
    for a in range(PAGE_PAIRS):
        p = jnp.exp(blocks[a] - m_new)
        l_new = l_new + jnp.sum(p, axis=-1, keepdims=True)
        for h in range(H_B):
            vh = jnp.concatenate([head_rows(v_refs[2 * a + 1], h), head_rows(v_refs[2 * a], h)], axis=0)
            d = jnp.dot(p[hrows * h:hrows * (h + 1)], vh, preferred_element_type=F32)
            pv[h] = d if pv[h] is None else pv[h] + d
    acc_ref[...] = alpha * acc_ref[...] + jnp.concatenate(pv, axis=0)
    l_ref[...] = l_new
    m_ref[...] = m_new

    @pl.when(j == pl.num_programs(1) - 1)
    def _():
        o = acc_ref[...] / l_ref[...]
        g = g_ref[...]
        parts = []
        for h in range(H_B):
            oh = o[hrows * h:hrows * h + n_new] - sc_ref[0] * o[hrows * h + n_new:hrows * (h + 1)]
            parts.append(_subln(oh, g, sc_ref[1]))
        o_ref[0] = jnp.concatenate(parts, axis=-1) * _silu(z_ref[0])


def _diff_decode(pt_flat, scalars, n_pages, layer, q3, kn3, vn3, zb3, bias_dec, subln_g,
                 cache_k, cache_v):
    nb, n_new, _ = kn3.shape
    per_req = lambda w, cb: pl.BlockSpec((1, n_new, w), lambda b, j, pt, sc: (b, 0, cb))
    grid_spec = pltpu.PrefetchScalarGridSpec(
        num_scalar_prefetch=2,
        grid=(nb, n_pages // PAGES_PER_STEP),
        in_specs=[per_req(512, QB0 // N_GROUPS), per_req(512, 0), per_req(512, 0), per_req(512, 0),
                  pl.BlockSpec((3, DEC_ROWS, PAGE_SIZE), lambda b, j, pt, sc: (0, 0, 0)),
                  pl.BlockSpec((1, LANES), lambda b, j, pt, sc: (0, 0))]
                 + [_page_spec(layer, n_pages, p) for p in range(PAGES_PER_STEP)] * 2,
        out_specs=per_req(512, 0),
        scratch_shapes=[pltpu.VMEM((DEC_ROWS, LANES), F32), pltpu.VMEM((DEC_ROWS, LANES), F32),
                        pltpu.VMEM((DEC_ROWS, 1), F32), pltpu.VMEM((DEC_ROWS, 1), F32),
                        pltpu.VMEM((PAGE_SIZE, 512), F32)],
    )
    return pl.pallas_call(
        _diff_decode_kernel,
        grid_spec=grid_spec,
        out_shape=jax.ShapeDtypeStruct((nb, n_new, W_B), F32),
        compiler_params=pltpu.CompilerParams(
            dimension_semantics=("arbitrary", "arbitrary"), vmem_limit_bytes=VMEM_LIMIT),
        name="diff_decode",
    )(pt_flat, scalars, q3, kn3, vn3, zb3, bias_dec, subln_g,
      *([cache_k] * PAGES_PER_STEP), *([cache_v] * PAGES_PER_STEP))


def _merge_kernel(x_ref, oa_ref, ob_ref, ga_ref, gb_ref, wa_ref, wb_ref, wo_ref, fg_ref, y_ref,
                  *, final):
    a = jnp.dot(oa_ref[...].astype(BF16), wa_ref[...], preferred_element_type=F32)
    b = jnp.dot(ob_ref[...].astype(BF16), wb_ref[...], preferred_element_type=F32)
    merged = _sigmoid(ga_ref[...]) * a + _sigmoid(gb_ref[...]) * b
    y = x_ref[...] + jnp.dot(merged.astype(BF16), wo_ref[...], preferred_element_type=F32)
    if final:
        ms = jnp.mean(y * y, axis=-1, keepdims=True)
        y = (y * lax.rsqrt(ms + RMS_EPS)) * fg_ref[...]
    y_ref[...] = y


def _merge(x, oa, ob, ga, gb, wa, wb, wo, fg, final, tm):
    t = x.shape[0]
    row = lambda w: pl.BlockSpec((tm, w), lambda i: (i, 0))
    whole = lambda a: pl.BlockSpec(a.shape, lambda i: (0, 0))
    return pl.pallas_call(
        functools.partial(_merge_kernel, final=final),
        grid=(t // tm,),
        in_specs=[row(D_MODEL), row(W_A), row(W_B), row(D_MODEL), row(D_MODEL),
                  whole(wa), whole(wb), whole(wo), whole(fg)],
        out_specs=row(D_MODEL),
        out_shape=jax.ShapeDtypeStruct((t, D_MODEL), F32),
        compiler_params=pltpu.CompilerParams(
            dimension_semantics=("arbitrary",), vmem_limit_bytes=VMEM_LIMIT),
        name="merge",
    )(x, oa, ob, ga, gb, wa, wb, wo, fg)


def _bias_by_distance(rel_bias, n):
    d = jnp.arange(n)
    nf = jnp.maximum(d, 1).astype(F32)
    large = MAX_EXACT + (jnp.log(nf / MAX_EXACT) / math.log(MAX_DISTANCE / MAX_EXACT)
                         * (N_BUCKETS - MAX_EXACT)).astype(jnp.int32)
    large = jnp.minimum(large, N_BUCKETS - 1)
    return rel_bias[jnp.where(d < MAX_EXACT, d, large)].astype(F32)


def _strict_tri(n, upper):
    a = lax.broadcasted_iota(jnp.int32, (n, n), 0)
    b = lax.broadcasted_iota(jnp.int32, (n, n), 1)
    return ((b > a) if upper else (a > b)).astype(BF16)


def kernel(x_prompt, x_sample, cache_sb_k, cache_sb_v, cache_diff_k, cache_diff_v, page_table,
           norm_g, w_in, lambda_q1, lambda_k1, lambda_q2, lambda_k2, subln_g, w_up_a, w_up_b,
           w_out, rel_bias, final_norm_g):
    depth = w_in.shape[0]
    nb_p, t_p, _ = x_prompt.shape
    nb_s, n_new, _ = x_sample.shape
    n_pages = page_table.shape[1]
    n_pool = cache_sb_k.shape[1]
    assert nb_p == 1 and t_p % Q_BLK == 0 and Q_BLK == 2 * K_BLK and K_BLK >= MAX_DISTANCE
    assert t_p >= FAR_GROUP * K_BLK and FAR_GROUP % 2 == 0
    assert n_new * H_A == DEC_ROWS and n_new * 2 * H_B == DEC_ROWS
    assert n_pages % PAGES_PER_STEP == 0 and PAGE_SIZE >= MAX_DISTANCE

    xp = x_prompt.reshape(t_p, D_MODEL)
    xs = x_sample.reshape(nb_s * n_new, D_MODEL)
    pt_flat = page_table.reshape(-1).astype(jnp.int32)
    caches = [jnp.transpose(c, (0, 1, 3, 4, 2)).reshape(depth, n_pool, 512, PAGE_SIZE)
              for c in (cache_sb_k, cache_sb_v)]
    caches += [c.reshape(depth, n_pool, 512, PAGE_SIZE) for c in (cache_diff_k, cache_diff_v)]
    ks = lax.broadcasted_iota(jnp.int32, (K_BLK, K_BLK), 0)
    kj = lax.broadcasted_iota(jnp.int32, (K_BLK, K_BLK), 1)
    tri_p = -(kj >= ks).astype(BF16)
    tri_1 = _strict_tri(PAGE_SIZE, upper=False)
    tri_2 = _strict_tri(2 * PAGE_SIZE, upper=False)
    fg = final_norm_g.reshape(1, D_MODEL).astype(F32)

    n_tab = Q_BLK + K_BLK
    tab = _bias_by_distance(rel_bias, n_tab + 1)
    far = tab[-1]
    ext = jnp.transpose(tab[jnp.clip(jnp.arange(n_tab + Q_BLK) - (Q_BLK - 1), 0, n_tab)])
    ext_len = n_tab + Q_BLK
    circ = jnp.tile(ext, (1, K_BLK + 1))[:, :K_BLK * (ext_len - 1)].reshape(H_B, K_BLK, ext_len - 1)
    bias_near = jnp.stack([circ[:, :, off + Q_BLK - 1:off + 2 * Q_BLK - 1] for off in (-K_BLK, 0, K_BLK)],
                          axis=1)
    def sample_tile(extra):
        c0 = Q_BLK - 1 + extra
        per_head = jnp.transpose(circ[:, 0:PAGE_SIZE, c0:c0 + n_new], (0, 2, 1))
        return jnp.broadcast_to(per_head[:, None], (H_B, 2, n_new, PAGE_SIZE)).reshape(DEC_ROWS, PAGE_SIZE)

    bias_far = jnp.broadcast_to(far[:, None, None], (H_B, 2 * n_new, PAGE_SIZE)).reshape(DEC_ROWS, PAGE_SIZE)
    bias_dec = jnp.stack([sample_tile(0), sample_tile(PAGE_SIZE), bias_far]).astype(F32)

    rows_p = ([], [], [], [])
    rows_s = ([], [], [], [])
    for i in range(depth):
        lam_init = 0.8 - 0.6 * math.exp(-0.3 * i)
        lam = (jnp.exp(jnp.sum(lambda_q1[i].astype(F32) * lambda_k1[i].astype(F32)))
               - jnp.exp(jnp.sum(lambda_q2[i].astype(F32) * lambda_k2[i].astype(F32)))
               + lam_init)
        scalars = jnp.concatenate([jnp.stack([lam, jnp.asarray(1.0 - lam_init, F32)]), far]).astype(F32)
        g_in = norm_g[i].reshape(1, D_MODEL).astype(F32)
        w_bf = w_in[i].astype(BF16)
        wa, wb, wo = w_up_a[i].astype(BF16), w_up_b[i].astype(BF16), w_out[i].astype(BF16)
        sg = subln_g[i].reshape(1, LANES).astype(F32)
        final = i == depth - 1

        qk, vat, vbt, ka, va, kb, vb, za, zb, ga, gb = _inproj(xp, g_in, w_bf, True, K_BLK)
        oa = _sb_prompt(qk, vat, za, tri_p)
        ob = _diff_prompt(scalars, qk, vbt, zb, bias_near, sg)
        xp = _merge(xp, oa, ob, ga, gb, wa, wb, wo, fg, final, 512)
        for lst, a, hn in zip(rows_p, (ka, va, kb, vb), (H_A, H_A, H_B, H_B)):
            lst.append(a.reshape(nb_p, t_p, hn, 512 // hn))

        qk, ka, va, kb, vb, za, zb, ga, gb = _inproj(xs, g_in, w_bf, False, nb_s * n_new)
        r3 = lambda a: a.reshape(nb_s, n_new, a.shape[-1])
        oa = _sb_decode(pt_flat, n_pages, i, r3(qk), r3(ka), r3(va), r3(za), tri_1, tri_2,
                        caches[0], caches[1])
        ob = _diff_decode(pt_flat, scalars, n_pages, i, r3(qk), r3(kb), r3(vb), r3(zb), bias_dec, sg,
                          caches[2], caches[3])
        xs = _merge(xs, oa.reshape(-1, W_A), ob.reshape(-1, W_B), ga, gb, wa, wb, wo, fg, final,
                    nb_s * n_new)
        for lst, a, hn in zip(rows_s, (ka, va, kb, vb), (H_A, H_A, H_B, H_B)):
            lst.append(a.reshape(nb_s, n_new, hn, 512 // hn))

    y_prompt = xp.reshape(nb_p, t_p, D_MODEL)
    y_sample = xs.reshape(nb_s, n_new, D_MODEL)
    return (y_prompt, y_sample,
            jnp.stack(rows_p[0]), jnp.stack(rows_p[1]), jnp.stack(rows_p[2]), jnp.stack(rows_p[3]),
            jnp.stack(rows_s[0]), jnp.stack(rows_s[1]), jnp.stack(rows_s[2]), jnp.stack(rows_s[3]))
```
